```python
import math
import jax, jax.numpy as jnp
from jax import lax
import numpy as np

D_MODEL = 1024
BATCH = 8
SEQ = 4096
DEPTH = 2

CHUNK = 64
Q_BLOCK = 128
N_MIXERS = 2
ATTN_HEADS = 8
ATTN_HEAD_DIM = D_MODEL // (2 * ATTN_HEADS)
ATTN_V_DIM = 2 * ATTN_HEAD_DIM
ROPE_THETA = 10000.0
LAMBDA_STD = 0.1
D_RNN = D_MODEL
RG_BLOCK = 256
RG_HEADS = D_RNN // RG_BLOCK
CONV_WIDTH = 4
RG_C = 8.0
D_FF = 4 * D_MODEL
NORM_EPS = 1e-6
SUBLN_EPS = 1e-5

kernel_name = "hybrid_diffattn_rglru_streaming"


def rmsnorm(x, g, eps=NORM_EPS):
    xf = x.astype(jnp.float32)
    y = xf * lax.rsqrt(jnp.mean(xf * xf, axis=-1, keepdims=True) + eps)
    return (y * g.astype(jnp.float32)).astype(x.dtype)


def rope(t, positions):
    d = t.shape[-1]
    inv_freq = 1.0 / (ROPE_THETA ** (jnp.arange(0, d, 2, dtype=jnp.float32) / d))
    ang = positions.astype(jnp.float32)[:, None] * inv_freq[None, :]
    cos = jnp.cos(ang)[:, None, None, :]
    sin = jnp.sin(ang)[:, None, None, :]
    tf = t.astype(jnp.float32)
    t1, t2 = tf[..., : d // 2], tf[..., d // 2:]
    out = jnp.concatenate([t1 * cos - t2 * sin, t2 * cos + t1 * sin], axis=-1)
    return out.astype(t.dtype)


def diff_attention(h, w_qkv, w_o, lq1, lk1, lq2, lk2, subln_g, lambda_init):
    B, S, _ = h.shape
    H, d = ATTN_HEADS, ATTN_HEAD_DIM
    qkv = h @ w_qkv
    q, k, v = jnp.split(qkv, 3, axis=-1)
    positions = jnp.arange(S, dtype=jnp.int32)
    q = rope(q.reshape(B, S, H, 2, d), positions) * (d ** -0.5)
    k = rope(k.reshape(B, S, H, 2, d), positions)
    v = v.reshape(B, S, H, ATTN_V_DIM)
    lam = (jnp.exp(jnp.sum(lq1.astype(jnp.float32) * lk1.astype(jnp.float32)))
           - jnp.exp(jnp.sum(lq2.astype(jnp.float32) * lk2.astype(jnp.float32)))
           + lambda_init)
    n_blk = S // Q_BLOCK
    q_blocks = q.reshape(B, n_blk, Q_BLOCK, H, 2, d).transpose(1, 0, 2, 3, 4, 5)
    k_chunk = jnp.arange(S) // CHUNK

    def one_block(args):
        qb, bi = args
        q_chunk = (bi * Q_BLOCK + jnp.arange(Q_BLOCK)) // CHUNK
        allowed = k_chunk[None, :] <= q_chunk[:, None]
        s = jnp.einsum('bqhcd,bkhcd->bhcqk', qb, k).astype(jnp.float32)
        s = jnp.where(allowed[None, None, None], s, -jnp.inf)
        p = jax.nn.softmax(s, axis=-1)
        a = p[:, :, 0] - lam * p[:, :, 1]
        return jnp.einsum('bhqk,bkhe->bqhe', a.astype(v.dtype), v)

    o = lax.map(one_block, (q_blocks, jnp.arange(n_blk)))
    o = o.transpose(1, 0, 2, 3, 4).reshape(B, S, H, ATTN_V_DIM)
    o = rmsnorm(o, subln_g, SUBLN_EPS) * (1.0 - lambda_init)
    return o.reshape(B, S, H * ATTN_V_DIM) @ w_o


def causal_depthwise_conv(x, w, b):
    S = x.shape[1]
    xp = jnp.pad(x, ((0, 0), (CONV_WIDTH - 1, 0), (0, 0)))
    y = sum(xp[:, j:j + S] * w[j] for j in range(CONV_WIDTH))
    return y + b


def _lru_combine(left, right):
    a1, b1 = left
    a2, b2 = right
    return a1 * a2, a2 * b1 + b2


def recurrent_block(h, w_x, w_y, conv_w, conv_b, w_a, b_a, w_i, b_i, lam_param, w_o):
    B, S, _ = h.shape
    gate_branch = jax.nn.gelu(h @ w_y)
    xb = causal_depthwise_conv(h @ w_x, conv_w, conv_b)
    xg = xb.reshape(B, S, RG_HEADS, RG_BLOCK)
    r = jax.nn.sigmoid((jnp.einsum('bsnc,ncd->bsnd', xg, w_a).reshape(B, S, D_RNN) + b_a).astype(jnp.float32))
    i = jax.nn.sigmoid((jnp.einsum('bsnc,ncd->bsnd', xg, w_i).reshape(B, S, D_RNN) + b_i).astype(jnp.float32))
    log_a = -RG_C * r * jax.nn.softplus(-lam_param.astype(jnp.float32))
    a = jnp.exp(log_a)
    mult = jnp.sqrt(-jnp.expm1(2.0 * log_a))
    u = mult * (i * xb.astype(jnp.float32))
    _, hs = lax.associative_scan(_lru_combine, (a, u), axis=1)
    return (hs.astype(h.dtype) * gate_branch) @ w_o


def sqrelu_mlp(h, w1, w2):
    return jnp.square(jax.nn.relu(h @ w1)) @ w2


def setup_inputs(seed: int = 0) -> dict:
    key = jax.random.key(seed)
    ks = iter(jax.random.split(key, 32))
    n_attn = (DEPTH + 1) // 2
    n_rec = DEPTH // 2
    f32 = jnp.float32

    def nrm(shape, fan_in):
        return jax.random.normal(next(ks), shape, f32) * (fan_in ** -0.5)

    def gain(shape):
        return 1.0 + 0.02 * jax.random.normal(next(ks), shape, f32)

    x = jax.random.normal(next(ks), (BATCH, SEQ, D_MODEL), f32)
    mix_norm_g = gain((DEPTH, D_MODEL))
    mlp_norm_g = gain((DEPTH, D_MODEL))
    attn_w_qkv = nrm((n_attn, D_MODEL, 3 * D_MODEL), D_MODEL)
    attn_w_o = nrm((n_attn, ATTN_HEADS * ATTN_V_DIM, D_MODEL), ATTN_HEADS * ATTN_V_DIM)
    attn_lq1 = LAMBDA_STD * jax.random.normal(next(ks), (n_attn, ATTN_HEAD_DIM), f32)
    attn_lk1 = LAMBDA_STD * jax.random.normal(next(ks), (n_attn, ATTN_HEAD_DIM), f32)
    attn_lq2 = LAMBDA_STD * jax.random.normal(next(ks), (n_attn, ATTN_HEAD_DIM), f32)
    attn_lk2 = LAMBDA_STD * jax.random.normal(next(ks), (n_attn, ATTN_HEAD_DIM), f32)
    attn_subln_g = gain((n_attn, ATTN_V_DIM))
    rec_w_x = nrm((n_rec, D_MODEL, D_RNN), D_MODEL)
    rec_w_y = nrm((n_rec, D_MODEL, D_RNN), D_MODEL)
    rec_conv_w = nrm((n_rec, CONV_WIDTH, D_RNN), CONV_WIDTH)
    rec_conv_b = 0.01 * jax.random.normal(next(ks), (n_rec, D_RNN), f32)
    rec_w_a = nrm((n_rec, RG_HEADS, RG_BLOCK, RG_BLOCK), RG_BLOCK)
    rec_b_a = 0.01 * jax.random.normal(next(ks), (n_rec, D_RNN), f32)
    rec_w_i = nrm((n_rec, RG_HEADS, RG_BLOCK, RG_BLOCK), RG_BLOCK)
    rec_b_i = 0.01 * jax.random.normal(next(ks), (n_rec, D_RNN), f32)
    u = jax.random.uniform(next(ks), (n_rec, D_RNN), f32, 0.9, 0.999)
    a_base = u ** (1.0 / RG_C)
    rec_lambda = jnp.log(a_base) - jnp.log1p(-a_base)
    rec_w_o = nrm((n_rec, D_RNN, D_MODEL), D_RNN)
    mlp_w1 = nrm((DEPTH, D_MODEL, D_FF), D_MODEL)
    mlp_w2 = nrm((DEPTH, D_FF, D_MODEL), D_FF)
    final_norm_g = gain((D_MODEL,))
    return {"x": x, "mix_norm_g": mix_norm_g, "mlp_norm_g": mlp_norm_g,
            "attn_w_qkv": attn_w_qkv, "attn_w_o": attn_w_o,
            "attn_lq1": attn_lq1, "attn_lk1": attn_lk1, "attn_lq2": attn_lq2, "attn_lk2": attn_lk2,
            "attn_subln_g": attn_subln_g,
            "rec_w_x": rec_w_x, "rec_w_y": rec_w_y, "rec_conv_w": rec_conv_w, "rec_conv_b": rec_conv_b,
            "rec_w_a": rec_w_a, "rec_b_a": rec_b_a, "rec_w_i": rec_w_i, "rec_b_i": rec_b_i,
            "rec_lambda": rec_lambda, "rec_w_o": rec_w_o,
            "mlp_w1": mlp_w1, "mlp_w2": mlp_w2, "final_norm_g": final_norm_g}


def reference(x, mix_norm_g, mlp_norm_g, attn_w_qkv, attn_w_o, attn_lq1, attn_lk1, attn_lq2, attn_lk2,
              attn_subln_g, rec_w_x, rec_w_y, rec_conv_w, rec_conv_b, rec_w_a, rec_b_a, rec_w_i, rec_b_i,
              rec_lambda, rec_w_o, mlp_w1, mlp_w2, final_norm_g):
    for layer in range(DEPTH):
        h = rmsnorm(x, mix_norm_g[layer])
        j = layer // N_MIXERS
        if layer % N_MIXERS == 0:
            lambda_init = 0.8 - 0.6 * math.exp(-0.3 * layer)
            x = x + diff_attention(h, attn_w_qkv[j], attn_w_o[j], attn_lq1[j], attn_lk1[j],
                                   attn_lq2[j], attn_lk2[j], attn_subln_g[j], lambda_init)
        else:
            x = x + recurrent_block(h, rec_w_x[j], rec_w_y[j], rec_conv_w[j], rec_conv_b[j],
                                    rec_w_a[j], rec_b_a[j], rec_w_i[j], rec_b_i[j],
                                    rec_lambda[j], rec_w_o[j])
        x = x + sqrelu_mlp(rmsnorm(x, mlp_norm_g[layer]), mlp_w1[layer], mlp_w2[layer])
    return rmsnorm(x, final_norm_g)
```

```python
import functools
import math

import jax
import jax.numpy as jnp
from jax import lax
from jax.experimental import pallas as pl
from jax.experimental.pallas import tpu as pltpu

F32 = jnp.float32
BF16 = jnp.bfloat16

D_MODEL = 1024
CHUNK = 64
ATTN_HEADS = 8
HEAD_DIM = 64
V_DIM = 128
ROPE_THETA = 10000.0
RG_BLOCK = 256
RG_HEADS = 4
CONV_WIDTH = 4
RG_C = 8.0
D_FF = 4 * D_MODEL
NORM_EPS = 1e-6
SUBLN_EPS = 1e-5

LANES = 128
SUBLANES = 8
VMEM_LIMIT = 56 * 1024 * 1024

TM_QKV = 512
TQ = 256
TK = 256
TM_MLP = 512
FF_CHUNK = 1024
TM_REC = 256

NT = (((1,), (1,)), ((), ()))


def _rms(x, g, eps):
    ms = jnp.mean(x * x, axis=-1, keepdims=True)
    return (x * lax.rsqrt(ms + eps)) * g


def _qkv_kernel(x_ref, g_ref, wq_ref, wk_ref, wvt_ref, cos_ref, sin_ref, q_ref, k_ref, vt_ref):
    y = _rms(x_ref[...], g_ref[...], NORM_EPS).astype(BF16)
    cos = cos_ref[...]
    sin = sin_ref[...]

    def rope(t):
        return t * cos + pltpu.roll(t, 64, 1) * sin

    q = jnp.dot(y, wq_ref[...], preferred_element_type=F32)
    for h in range(ATTN_HEADS):
        sl = slice(h * LANES, (h + 1) * LANES)
        q_ref[:, sl] = (rope(q[:, sl]) * (HEAD_DIM ** -0.5)).astype(BF16)
    k = jnp.dot(y, wk_ref[...], preferred_element_type=F32)
    for h in range(ATTN_HEADS):
        sl = slice(h * LANES, (h + 1) * LANES)
        k_ref[:, sl] = rope(k[:, sl]).astype(BF16)
    vt = lax.dot_general(wvt_ref[...], y, NT, preferred_element_type=F32)
    for h in range(ATTN_HEADS):
        for t in range(TM_QKV // TK):
            vt_ref[0, h, t] = vt[h * V_DIM:(h + 1) * V_DIM, t * TK:(t + 1) * TK].astype(BF16)


def _qkv_rope(x2d, g, wq, wk, wvt, cos, sin, batch, seq):
    n_tok = x2d.shape[0]
    tiles_per_seq = seq // TM_QKV
    const = lambda i: (0, 0)
    return pl.pallas_call(
        _qkv_kernel,
        grid=(n_tok // TM_QKV,),
        in_specs=[
            pl.BlockSpec((TM_QKV, D_MODEL), lambda i: (i, 0)),
            pl.BlockSpec((1, D_MODEL), const),
            pl.BlockSpec((D_MODEL, D_MODEL), const),
            pl.BlockSpec((D_MODEL, D_MODEL), const),
            pl.BlockSpec((D_MODEL, D_MODEL), const),
            pl.BlockSpec((TM_QKV, LANES), lambda i: (i % tiles_per_seq, 0)),
            pl.BlockSpec((TM_QKV, LANES), lambda i: (i % tiles_per_seq, 0)),
        ],
        out_specs=[
            pl.BlockSpec((TM_QKV, D_MODEL), lambda i: (i, 0)),
            pl.BlockSpec((TM_QKV, D_MODEL), lambda i: (i, 0)),
            pl.BlockSpec((1, ATTN_HEADS, TM_QKV // TK, V_DIM, TK),
                         lambda i: (i // tiles_per_seq, 0, i % tiles_per_seq, 0, 0)),
        ],
        out_shape=[
            jax.ShapeDtypeStruct((n_tok, D_MODEL), BF16),
            jax.ShapeDtypeStruct((n_tok, D_MODEL), BF16),
            jax.ShapeDtypeStruct((batch, ATTN_HEADS, seq // TK, V_DIM, TK), BF16),
        ],
        compiler_params=pltpu.CompilerParams(
            dimension_semantics=("arbitrary",), vmem_limit_bytes=VMEM_LIMIT),
        name="qkv_rope",
    )(x2d, g, wq, wk, wvt, cos, sin)


def _attn_kernel(lqk_ref, g_ref, q_ref, k_ref, vt_ref, o_ref, m_ref, l_ref, acc_ref, *, lambda_init):
    qi = pl.program_id(2)

    q = q_ref[0].astype(F32)
    lane = lax.broadcasted_iota(jnp.int32, (TQ, LANES), 1)
    is_map1 = (lane // (HEAD_DIM // 2)) % 2 == 0
    q_st = jnp.concatenate(
        [jnp.where(is_map1, q, 0.0), jnp.where(is_map1, 0.0, q)], axis=0).astype(BF16)

    m_ref[...] = jnp.full(m_ref.shape, -1e30, F32)
    l_ref[...] = jnp.zeros(l_ref.shape, F32)
    acc_ref[...] = jnp.zeros(acc_ref.shape, F32)

    def tile(j, allowed):
        kt = k_ref[0, pl.ds(pl.multiple_of(j * TK, TK), TK), :]
        s = lax.dot_general(kt, q_st, NT, preferred_element_type=F32)
        if allowed is not None:
            s = jnp.where(allowed, s, -jnp.inf)
        m_old = m_ref[...]
        m_new = jnp.maximum(m_old, jnp.max(s, axis=0, keepdims=True))
        alpha = jnp.exp(m_old - m_new)
        p = jnp.exp(s - m_new)
        l_ref[...] = alpha * l_ref[...] + jnp.sum(p, axis=0, keepdims=True)
        pv = jnp.dot(vt_ref[0, 0, j], p.astype(BF16), preferred_element_type=F32)
        acc_ref[...] = alpha * acc_ref[...] + pv
        m_ref[...] = m_new

    def body(j, carry):
        tile(j, None)
        return carry

    lax.fori_loop(0, qi, body, 0)

    key_chunk = lax.broadcasted_iota(jnp.int32, (TK, 2 * TQ), 0) // CHUNK
    qry_chunk = (lax.broadcasted_iota(jnp.int32, (TK, 2 * TQ), 1) % TQ) // CHUNK
    tile(qi, key_chunk <= qry_chunk)

    lqk = lqk_ref[...]
    lam = (jnp.exp(jnp.sum(lqk[0:1] * lqk[1:2], axis=1, keepdims=True))
           - jnp.exp(jnp.sum(lqk[2:3] * lqk[3:4], axis=1, keepdims=True))
           + lambda_init)
    acc = acc_ref[...]
    l = l_ref[...]
    o = acc[:, :TQ] / l[:, :TQ] - lam * (acc[:, TQ:] / l[:, TQ:])
    ms = jnp.mean(o * o, axis=0, keepdims=True)
    y = ((o * lax.rsqrt(ms + SUBLN_EPS)) * g_ref[...]) * (1.0 - lambda_init)
    o_ref[0] = y.T.astype(BF16)


def _diff_attn(lqk, g_col, q, k, vt, lambda_init):
    batch, seq, _ = q.shape
    kernel = functools.partial(_attn_kernel, lambda_init=lambda_init)
    return pl.pallas_call(
        kernel,
        grid=(batch, ATTN_HEADS, seq // TQ),
        in_specs=[
            pl.BlockSpec((4, HEAD_DIM), lambda b, h, i: (0, 0)),
            pl.BlockSpec((V_DIM, 1), lambda b, h, i: (0, 0)),
            pl.BlockSpec((1, TQ, LANES), lambda b, h, i: (b, i, h)),
            pl.BlockSpec((1, seq, LANES), lambda b, h, i: (b, 0, h)),
            pl.BlockSpec((1, 1, seq // TK, V_DIM, TK), lambda b, h, i: (b, h, 0, 0, 0)),
        ],
        out_specs=pl.BlockSpec((1, TQ, LANES), lambda b, h, i: (b, i, h)),
        out_shape=jax.ShapeDtypeStruct((batch, seq, ATTN_HEADS * V_DIM), BF16),
        scratch_shapes=[
            pltpu.VMEM((1, 2 * TQ), F32),
            pltpu.VMEM((1, 2 * TQ), F32),
            pltpu.VMEM((V_DIM, 2 * TQ), F32),
        ],
        compiler_params=pltpu.CompilerParams(
            dimension_semantics=("arbitrary", "arbitrary", "arbitrary"),
            vmem_limit_bytes=VMEM_LIMIT),
        name="diff_attn",
    )(lqk, g_col, q, k, vt)


def _proj_mlp_kernel(x_ref, a_ref, wo_ref, g_ref, w1_ref, w2_ref, gf_ref, out_ref, *, final_norm):
    x1 = x_ref[...] + jnp.dot(a_ref[...], wo_ref[...], preferred_element_type=F32)
    h = _rms(x1, g_ref[...], NORM_EPS).astype(BF16)
    acc = x1
    for c in range(D_FF // FF_CHUNK):
        sl = slice(c * FF_CHUNK, (c + 1) * FF_CHUNK)
        u = jnp.dot(h, w1_ref[:, sl], preferred_element_type=F32)
        u = jnp.square(jnp.maximum(u, 0.0)).astype(BF16)
        acc = acc + jnp.dot(u, w2_ref[sl, :], preferred_element_type=F32)
    if final_norm:
        acc = _rms(acc, gf_ref[...], NORM_EPS)
    out_ref[...] = acc


def _proj_mlp(x2d, a2d, wo, g, w1, w2, gf, final_norm):
    n_tok = x2d.shape[0]
    const = lambda i: (0, 0)
    resident = pl.Buffered(1)
    kernel = functools.partial(_proj_mlp_kernel, final_norm=final_norm)
    return pl.pallas_call(
        kernel,
        grid=(n_tok // TM_MLP,),
        in_specs=[
            pl.BlockSpec((TM_MLP, D_MODEL), lambda i: (i, 0)),
            pl.BlockSpec((TM_MLP, D_MODEL), lambda i: (i, 0)),
            pl.BlockSpec((D_MODEL, D_MODEL), const, pipeline_mode=resident),
            pl.BlockSpec((1, D_MODEL), const),
            pl.BlockSpec((D_MODEL, D_FF), const, pipeline_mode=resident),
            pl.BlockSpec((D_FF, D_MODEL), const, pipeline_mode=resident),
            pl.BlockSpec((1, D_MODEL), const),
        ],
        out_specs=pl.BlockSpec((TM_MLP, D_MODEL), lambda i: (i, 0)),
        out_shape=jax.ShapeDtypeStruct((n_tok, D_MODEL), F32),
        compiler_params=pltpu.CompilerParams(
            dimension_semantics=("arbitrary",), vmem_limit_bytes=VMEM_LIMIT),
        name="proj_mlp_final" if final_norm else "proj_mlp",
    )(x2d, a2d, wo, g, w1, w2, gf)


def _rglru_kernel(x_ref, g_ref, wx_ref, wy_ref, cw_ref, cb_ref, wa_ref, ba_ref, wi_ref, bi_ref,
                  lam_ref, out_ref, conv_ref, h_ref):
    t = pl.program_id(1)

    @pl.when(t == 0)
    def _():
        conv_ref[...] = jnp.zeros(conv_ref.shape, F32)
        h_ref[...] = jnp.zeros(h_ref.shape, F32)

    h = _rms(x_ref[0], g_ref[...], NORM_EPS).astype(BF16)
    gate = jax.nn.gelu(jnp.dot(h, wy_ref[...], preferred_element_type=F32))
    xp = jnp.dot(h, wx_ref[...], preferred_element_type=F32)

    conv_ref[0:SUBLANES, :] = conv_ref[TM_REC:TM_REC + SUBLANES, :]
    conv_ref[SUBLANES:, :] = xp
    cw = cw_ref[...]
    xb = cb_ref[...]
    for j in range(CONV_WIDTH):
        off = SUBLANES - (CONV_WIDTH - 1) + j
        xb = xb + conv_ref[off:off + TM_REC, :] * cw[j:j + 1, :]

    xb16 = xb.astype(BF16)

    def block_diag(w_ref):
        return jnp.concatenate(
            [jnp.dot(xb16[:, n * RG_BLOCK:(n + 1) * RG_BLOCK], w_ref[n], preferred_element_type=F32)
             for n in range(RG_HEADS)], axis=1)

    r = jax.nn.sigmoid(block_diag(wa_ref) + ba_ref[...])
    i = jax.nn.sigmoid(block_diag(wi_ref) + bi_ref[...])
    neg_lam = -lam_ref[...]
    softplus = jnp.maximum(neg_lam, 0.0) + jnp.log1p(jnp.exp(-jnp.abs(neg_lam)))
    log_a = (-RG_C * r) * softplus
    a = jnp.exp(log_a)
    u = jnp.sqrt(1.0 - a * a) * (i * xb)

    row = lax.broadcasted_iota(jnp.int32, (TM_REC, D_MODEL), 0)
    d = 1
    while d < TM_REC:
        keep = row >= d
        u = u + a * jnp.where(keep, pltpu.roll(u, d, 0), 0.0)
        a = a * jnp.where(keep, pltpu.roll(a, d, 0), 1.0)
        d *= 2
    hs = u + a * h_ref[...]
    h_ref[...] = hs[TM_REC - 1:TM_REC, :]
    out_ref[0] = (hs * gate).astype(BF16)


def _rglru(x, g, wx, wy, cw, cb, wa, ba, wi, bi, lam):
    batch, seq, _ = x.shape
    c2 = lambda b, t: (0, 0)
    c3 = lambda b, t: (0, 0, 0)
    return pl.pallas_call(
        _rglru_kernel,
        grid=(batch, seq // TM_REC),
        in_specs=[
            pl.BlockSpec((1, TM_REC, D_MODEL), lambda b, t: (b, t, 0)),
            pl.BlockSpec((1, D_MODEL), c2),
            pl.BlockSpec((D_MODEL, D_MODEL), c2),
            pl.BlockSpec((D_MODEL, D_MODEL), c2),
            pl.BlockSpec((CONV_WIDTH, D_MODEL), c2),
            pl.BlockSpec((1, D_MODEL), c2),
            pl.BlockSpec((RG_HEADS, RG_BLOCK, RG_BLOCK), c3),
            pl.BlockSpec((1, D_MODEL), c2),
            pl.BlockSpec((RG_HEADS, RG_BLOCK, RG_BLOCK), c3),
            pl.BlockSpec((1, D_MODEL), c2),
            pl.BlockSpec((1, D_MODEL), c2),
        ],
        out_specs=pl.BlockSpec((1, TM_REC, D_MODEL), lambda b, t: (b, t, 0)),
        out_shape=jax.ShapeDtypeStruct((batch, seq, D_MODEL), BF16),
        scratch_shapes=[
            pltpu.VMEM((TM_REC + SUBLANES, D_MODEL), F32),
            pltpu.VMEM((1, D_MODEL), F32),
        ],
        compiler_params=pltpu.CompilerParams(
            dimension_semantics=("arbitrary", "arbitrary"), vmem_limit_bytes=VMEM_LIMIT),
        name="rglru",
    )(x, g, wx, wy, cw, cb, wa, ba, wi, bi, lam)


def _rope_tables(seq):
    half = HEAD_DIM // 2
    inv_freq = 1.0 / (ROPE_THETA ** (jnp.arange(0, HEAD_DIM, 2, dtype=F32) / HEAD_DIM))
    ang = jnp.arange(seq, dtype=jnp.int32).astype(F32)[:, None] * inv_freq[None, :]
    cos = jnp.tile(jnp.cos(ang), (1, LANES // half))
    sin = jnp.tile(jnp.sin(ang), (1, LANES // half))
    sign = jnp.where(jnp.arange(LANES) < LANES // 2, -1.0, 1.0).astype(F32)
    return cos, sin * sign[None, :]


def _permute_heads(w):
    d_in = w.shape[0]
    half = HEAD_DIM // 2
    return w.reshape(d_in, ATTN_HEADS, 2, 2, half).transpose(0, 1, 3, 2, 4).reshape(d_in, -1)


def kernel(x, mix_norm_g, mlp_norm_g, attn_w_qkv, attn_w_o, attn_lq1, attn_lk1, attn_lq2, attn_lk2,
           attn_subln_g, rec_w_x, rec_w_y, rec_conv_w, rec_conv_b, rec_w_a, rec_b_a, rec_w_i, rec_b_i,
           rec_lambda, rec_w_o, mlp_w1, mlp_w2, final_norm_g):
    batch, seq, d = x.shape
    x2d = x.reshape(batch * seq, d)
    row = lambda v: v.reshape(1, -1)

    lambda_init = 0.8 - 0.6 * math.exp(-0.3 * 0)
    w_qkv = attn_w_qkv[0]
    wq = _permute_heads(w_qkv[:, :d]).astype(BF16)
    wk = _permute_heads(w_qkv[:, d:2 * d]).astype(BF16)
    wvt = w_qkv[:, 2 * d:].T.astype(BF16)
    cos, sin = _rope_tables(seq)
    q, k, vt = _qkv_rope(x2d, row(mix_norm_g[0]), wq, wk, wvt, cos, sin, batch, seq)
    lqk = jnp.stack([attn_lq1[0], attn_lk1[0], attn_lq2[0], attn_lk2[0]])
    o = _diff_attn(lqk, attn_subln_g[0].reshape(V_DIM, 1),
                   q.reshape(batch, seq, d), k.reshape(batch, seq, d), vt, lambda_init)
    x2d = _proj_mlp(x2d, o.reshape(batch * seq, d), attn_w_o[0].astype(BF16), row(mlp_norm_g[0]),
                    mlp_w1[0].astype(BF16), mlp_w2[0].astype(BF16), row(final_norm_g), False)

    gated = _rglru(x2d.reshape(batch, seq, d), row(mix_norm_g[1]),
                   rec_w_x[0].astype(BF16), rec_w_y[0].astype(BF16), rec_conv_w[0], row(rec_conv_b[0]),
                   rec_w_a[0].astype(BF16), row(rec_b_a[0]), rec_w_i[0].astype(BF16), row(rec_b_i[0]),
                   row(rec_lambda[0]))
    out = _proj_mlp(x2d, gated.reshape(batch * seq, d), rec_w_o[0].astype(BF16), row(mlp_norm_g[1]),
                    mlp_w1[1].astype(BF16), mlp_w2[1].astype(BF16), row(final_norm_g), True)
    return out.reshape(batch, seq, d)
```

```python
import functools
import math

import jax
import jax.numpy as jnp
from jax import lax
from jax.experimental import pallas as pl
from jax.experimental.pallas import tpu as pltpu

F32 = jnp.float32
BF16 = jnp.bfloat16

D_MODEL = 1024
CHUNK = 64
ATTN_HEADS = 8
HEAD_DIM = 64
V_DIM = 128
ROPE_THETA = 10000.0
RG_BLOCK = 256
RG_HEADS = 4
CONV_WIDTH = 4
RG_C = 8.0
D_FF = 4 * D_MODEL
NORM_EPS = 1e-6
SUBLN_EPS = 1e-5

LANES = 128
SUBLANES = 8
VMEM_LIMIT = 56 * 1024 * 1024

TM_QKV = 512
TQ = 256
TK = 512
N_STREAMS = 4
TM_MLP = 512
FF_CHUNK = 1024
TM_REC = 256

NT = (((1,), (1,)), ((), ()))
Q_SCALE = (HEAD_DIM ** -0.5) * math.log2(math.e)


def _rms(x, g, eps):
    ms = jnp.mean(x * x, axis=-1, keepdims=True)
    return (x * lax.rsqrt(ms + eps)) * g


def _qkv_kernel(x_ref, g_ref, wq_ref, wk_ref, wvt_ref, cos_ref, sin_ref, q_ref, k_ref, vt_ref):
    y = _rms(x_ref[...], g_ref[...], NORM_EPS).astype(BF16)
    cos = cos_ref[...]
    sin = sin_ref[...]

    def rope(t):
        return t * cos + pltpu.roll(t, 64, 1) * sin

    q = jnp.dot(y, wq_ref[...], preferred_element_type=F32)
    for h in range(ATTN_HEADS):
        sl = slice(h * LANES, (h + 1) * LANES)
        q_ref[:, sl] = (rope(q[:, sl]) * Q_SCALE).astype(BF16)
    k = jnp.dot(y, wk_ref[...], preferred_element_type=F32)
    for h in range(ATTN_HEADS):
        sl = slice(h * LANES, (h + 1) * LANES)
        k_ref[:, sl] = rope(k[:, sl]).astype(BF16)
    vt = lax.dot_general(wvt_ref[...], y, NT, preferred_element_type=F32)
    for h in range(ATTN_HEADS):
        for t in range(TM_QKV // TK):
            vt_ref[0, h, t] = vt[h * V_DIM:(h + 1) * V_DIM, t * TK:(t + 1) * TK].astype(BF16)


def _qkv_rope(x2d, g, wq, wk, wvt, cos, sin, batch, seq):
    n_tok = x2d.shape[0]
    tiles_per_seq = seq // TM_QKV
    const = lambda i: (0, 0)
    return pl.pallas_call(
        _qkv_kernel,
        grid=(n_tok // TM_QKV,),
        in_specs=[
            pl.BlockSpec((TM_QKV, D_MODEL), lambda i: (i, 0)),
            pl.BlockSpec((1, D_MODEL), const),
            pl.BlockSpec((D_MODEL, D_MODEL), const),
            pl.BlockSpec((D_MODEL, D_MODEL), const),
            pl.BlockSpec((D_MODEL, D_MODEL), const),
            pl.BlockSpec((TM_QKV, LANES), lambda i: (i % tiles_per_seq, 0)),
            pl.BlockSpec((TM_QKV, LANES), lambda i: (i % tiles_per_seq, 0)),
        ],
        out_specs=[
            pl.BlockSpec((TM_QKV, D_MODEL), lambda i: (i, 0)),
            pl.BlockSpec((TM_QKV, D_MODEL), lambda i: (i, 0)),
            pl.BlockSpec((1, ATTN_HEADS, TM_QKV // TK, V_DIM, TK),
                         lambda i: (i // tiles_per_seq, 0, i % tiles_per_seq, 0, 0)),
        ],
        out_shape=[
            jax.ShapeDtypeStruct((n_tok, D_MODEL), BF16),
            jax.ShapeDtypeStruct((n_tok, D_MODEL), BF16),
            jax.ShapeDtypeStruct((batch, ATTN_HEADS, seq // TK, V_DIM, TK), BF16),
        ],
        compiler_params=pltpu.CompilerParams(
            dimension_semantics=("arbitrary",), vmem_limit_bytes=VMEM_LIMIT),
        name="qkv_rope",
    )(x2d, g, wq, wk, wvt, cos, sin)


def _attn_kernel(lqk_ref, g_ref, q_ref, k_ref, vt_ref, o_ref, m_ref, l_ref, acc_ref, *, lambda_init):
    qi = pl.program_id(2)

    lane = lax.broadcasted_iota(jnp.int32, (TQ, LANES), 1)
    is_map1 = (lane // (HEAD_DIM // 2)) % 2 == 0
    q_st = []
    for s in range(N_STREAMS):
        q = q_ref[0, :, s * LANES:(s + 1) * LANES].astype(F32)
        q_st.append(jnp.concatenate(
            [jnp.where(is_map1, q, 0.0), jnp.where(is_map1, 0.0, q)], axis=0).astype(BF16))

    m_ref[...] = jnp.full(m_ref.shape, -1e30, F32)
    l_ref[...] = jnp.zeros(l_ref.shape, F32)
    acc_ref[...] = jnp.zeros(acc_ref.shape, F32)

    def tiles(j, allowed):
        rows = pl.ds(pl.multiple_of(j * TK, TK), TK)
        scores = [lax.dot_general(k_ref[0, rows, s * LANES:(s + 1) * LANES], q_st[s], NT,
                                  preferred_element_type=F32) for s in range(N_STREAMS)]
        probs, alphas = [], []
        for s in range(N_STREAMS):
            sc = scores[s]
            if allowed is not None:
                sc = jnp.where(allowed, sc, -jnp.inf)
            m_old = m_ref[s]
            m_new = jnp.maximum(m_old, jnp.max(sc, axis=0, keepdims=True))
            alpha = jnp.exp2(m_old - m_new)
            p = jnp.exp2(sc - m_new)
            l_ref[s] = alpha * l_ref[s] + jnp.sum(p, axis=0, keepdims=True)
            m_ref[s] = m_new
            probs.append(p.astype(BF16))
            alphas.append(alpha)
        for s in range(N_STREAMS):
            pv = jnp.dot(vt_ref[0, s, j], probs[s], preferred_element_type=F32)
            acc_ref[s] = alphas[s] * acc_ref[s] + pv

    def body(j, carry):
        tiles(j, None)
        return carry

    q_per_k = TK // TQ
    last = qi // q_per_k
    lax.fori_loop(0, last, body, 0)

    key_chunk = lax.broadcasted_iota(jnp.int32, (TK, 2 * TQ), 0) // CHUNK
    qry_chunk = (lax.broadcasted_iota(jnp.int32, (TK, 2 * TQ), 1) % TQ) // CHUNK
    qry_offset = (qi % q_per_k) * (TQ // CHUNK)
    tiles(last, key_chunk - qry_chunk <= qry_offset)

    lqk = lqk_ref[...]
    lam = (jnp.exp(jnp.sum(lqk[0:1] * lqk[1:2], axis=1, keepdims=True))
           - jnp.exp(jnp.sum(lqk[2:3] * lqk[3:4], axis=1, keepdims=True))
           + lambda_init)
    for s in range(N_STREAMS):
        acc = acc_ref[s]
        l = l_ref[s]
        o = acc[:, :TQ] / l[:, :TQ] - lam * (acc[:, TQ:] / l[:, TQ:])
        ms = jnp.mean(o * o, axis=0, keepdims=True)
        y = ((o * lax.rsqrt(ms + SUBLN_EPS)) * g_ref[...]) * (1.0 - lambda_init)
        o_ref[0, :, s * LANES:(s + 1) * LANES] = y.T.astype(BF16)


def _diff_attn(lqk, g_col, q, k, vt, lambda_init):
    batch, seq, _ = q.shape
    width = N_STREAMS * LANES
    kernel = functools.partial(_attn_kernel, lambda_init=lambda_init)
    return pl.pallas_call(
        kernel,
        grid=(batch, ATTN_HEADS // N_STREAMS, seq // TQ),
        in_specs=[
            pl.BlockSpec((4, HEAD_DIM), lambda b, h, i: (0, 0)),
            pl.BlockSpec((V_DIM, 1), lambda b, h, i: (0, 0)),
            pl.BlockSpec((1, TQ, width), lambda b, h, i: (b, i, h)),
            pl.BlockSpec((1, seq, width), lambda b, h, i: (b, 0, h)),
            pl.BlockSpec((1, N_STREAMS, seq // TK, V_DIM, TK), lambda b, h, i: (b, h, 0, 0, 0)),
        ],
        out_specs=pl.BlockSpec((1, TQ, width), lambda b, h, i: (b, i, h)),
        out_shape=jax.ShapeDtypeStruct((batch, seq, ATTN_HEADS * V_DIM), BF16),
        scratch_shapes=[
            pltpu.VMEM((N_STREAMS, 1, 2 * TQ), F32),
            pltpu.VMEM((N_STREAMS, 1, 2 * TQ), F32),
            pltpu.VMEM((N_STREAMS, V_DIM, 2 * TQ), F32),
        ],
        compiler_params=pltpu.CompilerParams(
            dimension_semantics=("arbitrary", "arbitrary", "arbitrary"),
            vmem_limit_bytes=VMEM_LIMIT),
        name="diff_attn",
    )(lqk, g_col, q, k, vt)


def _proj_mlp_kernel(x_ref, a_ref, wo_ref, g_ref, w1_ref, w2_ref, gf_ref, out_ref, *, final_norm):
    x1 = x_ref[...] + jnp.dot(a_ref[...], wo_ref[...], preferred_element_type=F32)
    h = _rms(x1, g_ref[...], NORM_EPS).astype(BF16)
    acc = x1
    for c in range(D_FF // FF_CHUNK):
        sl = slice(c * FF_CHUNK, (c + 1) * FF_CHUNK)
        u = jnp.dot(h, w1_ref[:, sl], preferred_element_type=F32)
        u = jnp.square(jnp.maximum(u, 0.0)).astype(BF16)
        acc = acc + jnp.dot(u, w2_ref[sl, :], preferred_element_type=F32)
    if final_norm:
        acc = _rms(acc, gf_ref[...], NORM_EPS)
    out_ref[...] = acc


def _proj_mlp(x2d, a2d, wo, g, w1, w2, gf, final_norm):
    n_tok = x2d.shape[0]
    const = lambda i: (0, 0)
    resident = pl.Buffered(1)
    kernel = functools.partial(_proj_mlp_kernel, final_norm=final_norm)
    return pl.pallas_call(
        kernel,
        grid=(n_tok // TM_MLP,),
        in_specs=[
            pl.BlockSpec((TM_MLP, D_MODEL), lambda i: (i, 0)),
            pl.BlockSpec((TM_MLP, D_MODEL), lambda i: (i, 0)),
            pl.BlockSpec((D_MODEL, D_MODEL), const, pipeline_mode=resident),
            pl.BlockSpec((1, D_MODEL), const),
            pl.BlockSpec((D_MODEL, D_FF), const, pipeline_mode=resident),
            pl.BlockSpec((D_FF, D_MODEL), const, pipeline_mode=resident),
            pl.BlockSpec((1, D_MODEL), const),
        ],
        out_specs=pl.BlockSpec((TM_MLP, D_MODEL), lambda i: (i, 0)),
        out_shape=jax.ShapeDtypeStruct((n_tok, D_MODEL), F32),
        compiler_params=pltpu.CompilerParams(
            dimension_semantics=("arbitrary",), vmem_limit_bytes=VMEM_LIMIT),
        name="proj_mlp_final" if final_norm else "proj_mlp",
    )(x2d, a2d, wo, g, w1, w2, gf)


def _rglru_kernel(x_ref, g_ref, wx_ref, wy_ref, cw_ref, cb_ref, wa_ref, ba_ref, wi_ref, bi_ref,
                  lam_ref, out_ref, conv_ref, h_ref):
    t = pl.program_id(1)

    @pl.when(t == 0)
    def _():
        conv_ref[...] = jnp.zeros(conv_ref.shape, F32)
        h_ref[...] = jnp.zeros(h_ref.shape, F32)

    h = _rms(x_ref[0], g_ref[...], NORM_EPS).astype(BF16)
    gate = jax.nn.gelu(jnp.dot(h, wy_ref[...], preferred_element_type=F32))
    xp = jnp.dot(h, wx_ref[...], preferred_element_type=F32)

    conv_ref[0:SUBLANES, :] = conv_ref[TM_REC:TM_REC + SUBLANES, :]
    conv_ref[SUBLANES:, :] = xp
    cw = cw_ref[...]
    xb = cb_ref[...]
    for j in range(CONV_WIDTH):
        off = SUBLANES - (CONV_WIDTH - 1) + j
        xb = xb + conv_ref[off:off + TM_REC, :] * cw[j:j + 1, :]

    xb16 = xb.astype(BF16)

    def block_diag(w_ref):
        return jnp.concatenate(
            [jnp.dot(xb16[:, n * RG_BLOCK:(n + 1) * RG_BLOCK], w_ref[n], preferred_element_type=F32)
             for n in range(RG_HEADS)], axis=1)

    r = jax.nn.sigmoid(block_diag(wa_ref) + ba_ref[...])
    i = jax.nn.sigmoid(block_diag(wi_ref) + bi_ref[...])
    neg_lam = -lam_ref[...]
    softplus = jnp.maximum(neg_lam, 0.0) + jnp.log1p(jnp.exp(-jnp.abs(neg_lam)))
    log_a = (-RG_C * r) * softplus
    a = jnp.exp(log_a)
    u = jnp.sqrt(1.0 - a * a) * (i * xb)

    row = lax.broadcasted_iota(jnp.int32, (TM_REC, D_MODEL), 0)
    d = 1
    while d < TM_REC:
        keep = row >= d
        u = u + a * jnp.where(keep, pltpu.roll(u, d, 0), 0.0)
        a = a * jnp.where(keep, pltpu.roll(a, d, 0), 1.0)
        d *= 2
    hs = u + a * h_ref[...]
    h_ref[...] = hs[TM_REC - 1:TM_REC, :]
    out_ref[0] = (hs * gate).astype(BF16)


def _rglru(x, g, wx, wy, cw, cb, wa, ba, wi, bi, lam):
    batch, seq, _ = x.shape
    c2 = lambda b, t: (0, 0)
    c3 = lambda b, t: (0, 0, 0)
    return pl.pallas_call(
        _rglru_kernel,
        grid=(batch, seq // TM_REC),
        in_specs=[
            pl.BlockSpec((1, TM_REC, D_MODEL), lambda b, t: (b, t, 0)),
            pl.BlockSpec((1, D_MODEL), c2),
            pl.BlockSpec((D_MODEL, D_MODEL), c2),
            pl.BlockSpec((D_MODEL, D_MODEL), c2),
            pl.BlockSpec((CONV_WIDTH, D_MODEL), c2),
            pl.BlockSpec((1, D_MODEL), c2),
            pl.BlockSpec((RG_HEADS, RG_BLOCK, RG_BLOCK), c3),
            pl.BlockSpec((1, D_MODEL), c2),
            pl.BlockSpec((RG_HEADS, RG_BLOCK, RG_BLOCK), c3),
            pl.BlockSpec((1, D_MODEL), c2),
            pl.BlockSpec((1, D_MODEL), c2),
        ],
        out_specs=pl.BlockSpec((1, TM_REC, D_MODEL), lambda b, t: (b, t, 0)),
        out_shape=jax.ShapeDtypeStruct((batch, seq, D_MODEL), BF16),
        scratch_shapes=[
            pltpu.VMEM((TM_REC + SUBLANES, D_MODEL), F32),
            pltpu.VMEM((1, D_MODEL), F32),
        ],
        compiler_params=pltpu.CompilerParams(
            dimension_semantics=("arbitrary", "arbitrary"), vmem_limit_bytes=VMEM_LIMIT),
        name="rglru",
    )(x, g, wx, wy, cw, cb, wa, ba, wi, bi, lam)


def _rope_tables(seq):
    half = HEAD_DIM // 2
    inv_freq = 1.0 / (ROPE_THETA ** (jnp.arange(0, HEAD_DIM, 2, dtype=F32) / HEAD_DIM))
    ang = jnp.arange(seq, dtype=jnp.int32).astype(F32)[:, None] * inv_freq[None, :]
    cos = jnp.tile(jnp.cos(ang), (1, LANES // half))
    sin = jnp.tile(jnp.sin(ang), (1, LANES // half))
    sign = jnp.where(jnp.arange(LANES) < LANES // 2, -1.0, 1.0).astype(F32)
    return cos, sin * sign[None, :]


def _permute_heads(w):
    d_in = w.shape[0]
    half = HEAD_DIM // 2
    return w.reshape(d_in, ATTN_HEADS, 2, 2, half).transpose(0, 1, 3, 2, 4).reshape(d_in, -1)


def kernel(x, mix_norm_g, mlp_norm_g, attn_w_qkv, attn_w_o, attn_lq1, attn_lk1, attn_lq2, attn_lk2,
           attn_subln_g, rec_w_x, rec_w_y, rec_conv_w, rec_conv_b, rec_w_a, rec_b_a, rec_w_i, rec_b_i,
           rec_lambda, rec_w_o, mlp_w1, mlp_w2, final_norm_g):
    batch, seq, d = x.shape
    x2d = x.reshape(batch * seq, d)
    row = lambda v: v.reshape(1, -1)

    lambda_init = 0.8 - 0.6 * math.exp(-0.3 * 0)
    w_qkv = attn_w_qkv[0]
    wq = _permute_heads(w_qkv[:, :d]).astype(BF16)
    wk = _permute_heads(w_qkv[:, d:2 * d]).astype(BF16)
    wvt = w_qkv[:, 2 * d:].T.astype(BF16)
    cos, sin = _rope_tables(seq)
    q, k, vt = _qkv_rope(x2d, row(mix_norm_g[0]), wq, wk, wvt, cos, sin, batch, seq)
    lqk = jnp.stack([attn_lq1[0], attn_lk1[0], attn_lq2[0], attn_lk2[0]])
    o = _diff_attn(lqk, attn_subln_g[0].reshape(V_DIM, 1),
                   q.reshape(batch, seq, d), k.reshape(batch, seq, d), vt, lambda_init)
    x2d = _proj_mlp(x2d, o.reshape(batch * seq, d), attn_w_o[0].astype(BF16), row(mlp_norm_g[0]),
                    mlp_w1[0].astype(BF16), mlp_w2[0].astype(BF16), row(final_norm_g), False)

    gated = _rglru(x2d.reshape(batch, seq, d), row(mix_norm_g[1]),
                   rec_w_x[0].astype(BF16), rec_w_y[0].astype(BF16), rec_conv_w[0], row(rec_conv_b[0]),
                   rec_w_a[0].astype(BF16), row(rec_b_a[0]), rec_w_i[0].astype(BF16), row(rec_b_i[0]),
                   row(rec_lambda[0]))
    out = _proj_mlp(x2d, gated.reshape(batch * seq, d), rec_w_o[0].astype(BF16), row(mlp_norm_g[1]),
                    mlp_w1[1].astype(BF16), mlp_w2[1].astype(BF16), row(final_norm_g), True)
    return out.reshape(batch, seq, d)
```

```python
import functools
import math

import jax
import jax.numpy as jnp
from jax import lax
from jax.experimental import pallas as pl
from jax.experimental.pallas import tpu as pltpu

F32 = jnp.float32
BF16 = jnp.bfloat16

D_MODEL = 1024
CHUNK = 64
ATTN_HEADS = 8
HEAD_DIM = 64
V_DIM = 128
ROPE_THETA = 10000.0
RG_BLOCK = 256
RG_HEADS = 4
CONV_WIDTH = 4
RG_C = 8.0
D_FF = 4 * D_MODEL
NORM_EPS = 1e-6
SUBLN_EPS = 1e-5

LANES = 128
SUBLANES = 8
VMEM_LIMIT = 56 * 1024 * 1024

TM_QKV = 512
TQ = 512
TK = 512
N_STREAMS = 4
TM_MLP = 512
FF_CHUNK = 1024
TM_REC = 256
REC_STEPS = TM_REC // SUBLANES
REC_STREAMS = 2

NT = (((1,), (1,)), ((), ()))
Q_SCALE = (HEAD_DIM ** -0.5) * math.log2(math.e)


def _rms(x, g, eps):
    ms = jnp.mean(x * x, axis=-1, keepdims=True)
    return (x * lax.rsqrt(ms + eps)) * g


def _qkv_kernel(x_ref, g_ref, wq_ref, wk_ref, wvt_ref, cos_ref, sin_ref, q_ref, k_ref, vt_ref):
    y = _rms(x_ref[...], g_ref[...], NORM_EPS).astype(BF16)
    cos = cos_ref[...]
    sin = sin_ref[...]

    def rope(t):
        return t * cos + pltpu.roll(t, 64, 1) * sin

    q = jnp.dot(y, wq_ref[...], preferred_element_type=F32)
    for h in range(ATTN_HEADS):
        sl = slice(h * LANES, (h + 1) * LANES)
        q_ref[:, sl] = (rope(q[:, sl]) * Q_SCALE).astype(BF16)
    k = jnp.dot(y, wk_ref[...], preferred_element_type=F32)
    for h in range(ATTN_HEADS):
        sl = slice(h * LANES, (h + 1) * LANES)
        k_ref[:, sl] = rope(k[:, sl]).astype(BF16)
    vt = lax.dot_general(wvt_ref[...], y, NT, preferred_element_type=F32)
    for h in range(ATTN_HEADS):
        for t in range(TM_QKV // TK):
            vt_ref[0, h, t] = vt[h * V_DIM:(h + 1) * V_DIM, t * TK:(t + 1) * TK].astype(BF16)


def _qkv_rope(x2d, g, wq, wk, wvt, cos, sin, batch, seq):
    n_tok = x2d.shape[0]
    tiles_per_seq = seq // TM_QKV
    const = lambda i: (0, 0)
    return pl.pallas_call(
        _qkv_kernel,
        grid=(n_tok // TM_QKV,),
        in_specs=[
            pl.BlockSpec((TM_QKV, D_MODEL), lambda i: (i, 0)),
            pl.BlockSpec((1, D_MODEL), const),
            pl.BlockSpec((D_MODEL, D_MODEL), const),
            pl.BlockSpec((D_MODEL, D_MODEL), const),
            pl.BlockSpec((D_MODEL, D_MODEL), const),
            pl.BlockSpec((TM_QKV, LANES), lambda i: (i % tiles_per_seq, 0)),
            pl.BlockSpec((TM_QKV, LANES), lambda i: (i % tiles_per_seq, 0)),
        ],
        out_specs=[
            pl.BlockSpec((TM_QKV, D_MODEL), lambda i: (i, 0)),
            pl.BlockSpec((TM_QKV, D_MODEL), lambda i: (i, 0)),
            pl.BlockSpec((1, ATTN_HEADS, TM_QKV // TK, V_DIM, TK),
                         lambda i: (i // tiles_per_seq, 0, i % tiles_per_seq, 0, 0)),
        ],
        out_shape=[
            jax.ShapeDtypeStruct((n_tok, D_MODEL), BF16),
            jax.ShapeDtypeStruct((n_tok, D_MODEL), BF16),
            jax.ShapeDtypeStruct((batch, ATTN_HEADS, seq // TK, V_DIM, TK), BF16),
        ],
        compiler_params=pltpu.CompilerParams(
            dimension_semantics=("arbitrary",), vmem_limit_bytes=VMEM_LIMIT),
        name="qkv_rope",
    )(x2d, g, wq, wk, wvt, cos, sin)


def _attn_kernel(lqk_ref, g_ref, q_ref, k_ref, vt_ref, o_ref, m_ref, l_ref, acc_ref, *, lambda_init):
    qi = pl.program_id(2)

    lane = lax.broadcasted_iota(jnp.int32, (TQ, LANES), 1)
    is_map1 = (lane // (HEAD_DIM // 2)) % 2 == 0
    q_st = []
    for s in range(N_STREAMS):
        q = q_ref[0, :, s * LANES:(s + 1) * LANES].astype(F32)
        q_st.append(jnp.concatenate(
            [jnp.where(is_map1, q, 0.0), jnp.where(is_map1, 0.0, q)], axis=0).astype(BF16))

    m_ref[...] = jnp.full(m_ref.shape, -1e30, F32)
    l_ref[...] = jnp.zeros(l_ref.shape, F32)
    acc_ref[...] = jnp.zeros(acc_ref.shape, F32)

    def tiles(j, allowed):
        rows = pl.ds(pl.multiple_of(j * TK, TK), TK)
        scores = [lax.dot_general(k_ref[0, rows, s * LANES:(s + 1) * LANES], q_st[s], NT,
                                  preferred_element_type=F32) for s in range(N_STREAMS)]
        probs, alphas = [], []
        for s in range(N_STREAMS):
            sc = scores[s]
            if allowed is not None:
                sc = jnp.where(allowed, sc, -jnp.inf)
            m_old = m_ref[s]
            m_new = jnp.maximum(m_old, jnp.max(sc, axis=0, keepdims=True))
            alpha = jnp.exp2(m_old - m_new)
            p = jnp.exp2(sc - m_new)
            l_ref[s] = alpha * l_ref[s] + jnp.sum(p, axis=0, keepdims=True)
            m_ref[s] = m_new
            probs.append(p.astype(BF16))
            alphas.append(alpha)
        for s in range(N_STREAMS):
            pv = jnp.dot(vt_ref[0, s, j], probs[s], preferred_element_type=F32)
            acc_ref[s] = alphas[s] * acc_ref[s] + pv

    def body(j, carry):
        tiles(j, None)
        return carry

    q_per_k = TK // TQ
    last = qi // q_per_k
    lax.fori_loop(0, last, body, 0)

    key_chunk = lax.broadcasted_iota(jnp.int32, (TK, 2 * TQ), 0) // CHUNK
    qry_chunk = (lax.broadcasted_iota(jnp.int32, (TK, 2 * TQ), 1) % TQ) // CHUNK
    qry_offset = (qi % q_per_k) * (TQ // CHUNK)
    tiles(last, key_chunk - qry_chunk <= qry_offset)

    lqk = lqk_ref[...]
    lam = (jnp.exp(jnp.sum(lqk[0:1] * lqk[1:2], axis=1, keepdims=True))
           - jnp.exp(jnp.sum(lqk[2:3] * lqk[3:4], axis=1, keepdims=True))
           + lambda_init)
    for s in range(N_STREAMS):
        acc = acc_ref[s]
        l = l_ref[s]
        o = acc[:, :TQ] / l[:, :TQ] - lam * (acc[:, TQ:] / l[:, TQ:])
        ms = jnp.mean(o * o, axis=0, keepdims=True)
        y = ((o * lax.rsqrt(ms + SUBLN_EPS)) * g_ref[...]) * (1.0 - lambda_init)
        o_ref[0, :, s * LANES:(s + 1) * LANES] = y.T.astype(BF16)


def _diff_attn(lqk, g_col, q, k, vt, lambda_init):
    batch, seq, _ = q.shape
    width = N_STREAMS * LANES
    kernel = functools.partial(_attn_kernel, lambda_init=lambda_init)
    return pl.pallas_call(
        kernel,
        grid=(batch, ATTN_HEADS // N_STREAMS, seq // TQ),
        in_specs=[
            pl.BlockSpec((4, HEAD_DIM), lambda b, h, i: (0, 0)),
            pl.BlockSpec((V_DIM, 1), lambda b, h, i: (0, 0)),
            pl.BlockSpec((1, TQ, width), lambda b, h, i: (b, i, h)),
            pl.BlockSpec((1, seq, width), lambda b, h, i: (b, 0, h)),
            pl.BlockSpec((1, N_STREAMS, seq // TK, V_DIM, TK), lambda b, h, i: (b, h, 0, 0, 0)),
        ],
        out_specs=pl.BlockSpec((1, TQ, width), lambda b, h, i: (b, i, h)),
        out_shape=jax.ShapeDtypeStruct((batch, seq, ATTN_HEADS * V_DIM), BF16),
        scratch_shapes=[
            pltpu.VMEM((N_STREAMS, 1, 2 * TQ), F32),
            pltpu.VMEM((N_STREAMS, 1, 2 * TQ), F32),
            pltpu.VMEM((N_STREAMS, V_DIM, 2 * TQ), F32),
        ],
        compiler_params=pltpu.CompilerParams(
            dimension_semantics=("arbitrary", "arbitrary", "arbitrary"),
            vmem_limit_bytes=VMEM_LIMIT),
        name="diff_attn",
    )(lqk, g_col, q, k, vt)


def _proj_mlp_kernel(x_ref, a_ref, wo_ref, g_ref, w1_ref, w2_ref, gf_ref, out_ref, *, final_norm):
    x1 = x_ref[...] + jnp.dot(a_ref[...], wo_ref[...], preferred_element_type=F32)
    h = _rms(x1, g_ref[...], NORM_EPS).astype(BF16)
    acc = x1
    for c in range(D_FF // FF_CHUNK):
        sl = slice(c * FF_CHUNK, (c + 1) * FF_CHUNK)
        u = jnp.dot(h, w1_ref[:, sl], preferred_element_type=F32)
        u = jnp.square(jnp.maximum(u, 0.0)).astype(BF16)
        acc = acc + jnp.dot(u, w2_ref[sl, :], preferred_element_type=F32)
    if final_norm:
        acc = _rms(acc, gf_ref[...], NORM_EPS)
    out_ref[...] = acc


def _proj_mlp(x2d, a2d, wo, g, w1, w2, gf, final_norm):
    n_tok = x2d.shape[0]
    const = lambda i: (0, 0)
    resident = pl.Buffered(1)
    kernel = functools.partial(_proj_mlp_kernel, final_norm=final_norm)
    return pl.pallas_call(
        kernel,
        grid=(n_tok // TM_MLP,),
        in_specs=[
            pl.BlockSpec((TM_MLP, D_MODEL), lambda i: (i, 0)),
            pl.BlockSpec((TM_MLP, D_MODEL), lambda i: (i, 0)),
            pl.BlockSpec((D_MODEL, D_MODEL), const, pipeline_mode=resident),
            pl.BlockSpec((1, D_MODEL), const),
            pl.BlockSpec((D_MODEL, D_FF), const, pipeline_mode=resident),
            pl.BlockSpec((D_FF, D_MODEL), const, pipeline_mode=resident),
            pl.BlockSpec((1, D_MODEL), const),
        ],
        out_specs=pl.BlockSpec((TM_MLP, D_MODEL), lambda i: (i, 0)),
        out_shape=jax.ShapeDtypeStruct((n_tok, D_MODEL), F32),
        compiler_params=pltpu.CompilerParams(
            dimension_semantics=("arbitrary",), vmem_limit_bytes=VMEM_LIMIT),
        name="proj_mlp_final" if final_norm else "proj_mlp",
    )(x2d, a2d, wo, g, w1, w2, gf)


def _rglru_kernel(x_ref, g_ref, perm_ref, unperm_ref, wx_ref, wy_ref, cw_ref, cb_ref, wa_ref, ba_ref,
                  wi_ref, bi_ref, lam_ref, out_ref, tail_ref, h_ref):
    t = pl.program_id(1)
    n_tail = (CONV_WIDTH - 1) * SUBLANES

    @pl.when(t == 0)
    def _():
        tail_ref[...] = jnp.zeros(tail_ref.shape, F32)
        h_ref[...] = jnp.zeros(h_ref.shape, F32)

    group = lambda v, r: v[r * SUBLANES:(r + 1) * SUBLANES, :]
    sub = lax.broadcasted_iota(jnp.int32, (SUBLANES, D_MODEL), 0)

    def project(s):
        h = _rms(x_ref[s], g_ref[...], NORM_EPS).astype(BF16)
        h = jnp.dot(perm_ref[...], h, preferred_element_type=F32).astype(BF16)
        gate = jax.nn.gelu(jnp.dot(h, wy_ref[...], preferred_element_type=F32))
        xp = jnp.dot(h, wx_ref[...], preferred_element_type=F32)
        return gate, xp

    def conv_and_gates(s, xp):
        lead = []
        for k in range(CONV_WIDTH - 1):
            cur = group(xp, REC_STEPS - (CONV_WIDTH - 1) + k)
            prev_last = tail_ref[s, (k + 1) * SUBLANES - 1:(k + 1) * SUBLANES, :]
            lead.append(jnp.where(sub == 0, prev_last, pltpu.roll(cur, 1, 0)))
        tail_ref[s] = xp[TM_REC - n_tail:, :]
        ext = jnp.concatenate(lead + [xp], axis=0)
        cw = cw_ref[...]
        xb = cb_ref[...]
        for j in range(CONV_WIDTH):
            xb = xb + ext[j * SUBLANES:j * SUBLANES + TM_REC, :] * cw[j:j + 1, :]
        xb16 = xb.astype(BF16)

        def block_diag(w_ref):
            return jnp.concatenate(
                [jnp.dot(xb16[:, n * RG_BLOCK:(n + 1) * RG_BLOCK], w_ref[n],
                         preferred_element_type=F32) for n in range(RG_HEADS)], axis=1)

        return xb, block_diag(wa_ref), block_diag(wi_ref)

    def recur(s, gate, xb, ra, ri):
        r = jax.nn.sigmoid(ra + ba_ref[...])
        i = jax.nn.sigmoid(ri + bi_ref[...])
        neg_lam = -lam_ref[...]
        softplus = jnp.maximum(neg_lam, 0.0) + jnp.log1p(jnp.exp(-jnp.abs(neg_lam)))
        log_a = (-RG_C * r) * softplus
        a = jnp.exp(log_a)
        u = jnp.sqrt(1.0 - a * a) * (i * xb)
        hz, pz = [group(u, 0)], [group(a, 0)]
        for k in range(1, REC_STEPS):
            a_k = group(a, k)
            hz.append(a_k * hz[-1] + group(u, k))
            pz.append(a_k * pz[-1])
        start = h_ref[s]
        starts = []
        for b in range(SUBLANES):
            starts.append(start)
            start = hz[-1][b:b + 1, :] + pz[-1][b:b + 1, :] * start
        h_ref[s] = start
        starts = jnp.concatenate(starts, axis=0)
        y = jnp.concatenate([(hz[k] + pz[k] * starts) * group(gate, k) for k in range(REC_STEPS)],
                            axis=0).astype(BF16)
        out_ref[s] = jnp.dot(unperm_ref[...], y, preferred_element_type=F32).astype(BF16)

    streams = range(REC_STREAMS)
    proj = [project(s) for s in streams]
    mid = [conv_and_gates(s, proj[s][1]) for s in streams]
    for s in streams:
        recur(s, proj[s][0], *mid[s])


def _rglru(x, g, wx, wy, cw, cb, wa, ba, wi, bi, lam):
    batch, seq, _ = x.shape
    src = (jnp.arange(TM_REC) % SUBLANES) * REC_STEPS + jnp.arange(TM_REC) // SUBLANES
    perm = (src[:, None] == jnp.arange(TM_REC)[None, :]).astype(BF16)
    c2 = lambda b, t: (0, 0)
    c3 = lambda b, t: (0, 0, 0)
    return pl.pallas_call(
        _rglru_kernel,
        grid=(batch // REC_STREAMS, seq // TM_REC),
        in_specs=[
            pl.BlockSpec((REC_STREAMS, TM_REC, D_MODEL), lambda b, t: (b, t, 0)),
            pl.BlockSpec((1, D_MODEL), c2),
            pl.BlockSpec((TM_REC, TM_REC), c2),
            pl.BlockSpec((TM_REC, TM_REC), c2),
            pl.BlockSpec((D_MODEL, D_MODEL), c2),
            pl.BlockSpec((D_MODEL, D_MODEL), c2),
            pl.BlockSpec((CONV_WIDTH, D_MODEL), c2),
            pl.BlockSpec((1, D_MODEL), c2),
            pl.BlockSpec((RG_HEADS, RG_BLOCK, RG_BLOCK), c3),
            pl.BlockSpec((1, D_MODEL), c2),
            pl.BlockSpec((RG_HEADS, RG_BLOCK, RG_BLOCK), c3),
            pl.BlockSpec((1, D_MODEL), c2),
            pl.BlockSpec((1, D_MODEL), c2),
        ],
        out_specs=pl.BlockSpec((REC_STREAMS, TM_REC, D_MODEL), lambda b, t: (b, t, 0)),
        out_shape=jax.ShapeDtypeStruct((batch, seq, D_MODEL), BF16),
        scratch_shapes=[
            pltpu.VMEM((REC_STREAMS, (CONV_WIDTH - 1) * SUBLANES, D_MODEL), F32),
            pltpu.VMEM((REC_STREAMS, 1, D_MODEL), F32),
        ],
        compiler_params=pltpu.CompilerParams(
            dimension_semantics=("arbitrary", "arbitrary"), vmem_limit_bytes=VMEM_LIMIT),
        name="rglru",
    )(x, g, perm, perm.T, wx, wy, cw, cb, wa, ba, wi, bi, lam)


def _rope_tables(seq):
    half = HEAD_DIM // 2
    inv_freq = 1.0 / (ROPE_THETA ** (jnp.arange(0, HEAD_DIM, 2, dtype=F32) / HEAD_DIM))
    ang = jnp.arange(seq, dtype=jnp.int32).astype(F32)[:, None] * inv_freq[None, :]
    cos = jnp.tile(jnp.cos(ang), (1, LANES // half))
    sin = jnp.tile(jnp.sin(ang), (1, LANES // half))
    sign = jnp.where(jnp.arange(LANES) < LANES // 2, -1.0, 1.0).astype(F32)
    return cos, sin * sign[None, :]


def _permute_heads(w):
    d_in = w.shape[0]
    half = HEAD_DIM // 2
    return w.reshape(d_in, ATTN_HEADS, 2, 2, half).transpose(0, 1, 3, 2, 4).reshape(d_in, -1)


def kernel(x, mix_norm_g, mlp_norm_g, attn_w_qkv, attn_w_o, attn_lq1, attn_lk1, attn_lq2, attn_lk2,
           attn_subln_g, rec_w_x, rec_w_y, rec_conv_w, rec_conv_b, rec_w_a, rec_b_a, rec_w_i, rec_b_i,
           rec_lambda, rec_w_o, mlp_w1, mlp_w2, final_norm_g):
    batch, seq, d = x.shape
    x2d = x.reshape(batch * seq, d)
    row = lambda v: v.reshape(1, -1)

    lambda_init = 0.8 - 0.6 * math.exp(-0.3 * 0)
    w_qkv = attn_w_qkv[0]
    wq = _permute_heads(w_qkv[:, :d]).astype(BF16)
    wk = _permute_heads(w_qkv[:, d:2 * d]).astype(BF16)
    wvt = w_qkv[:, 2 * d:].T.astype(BF16)
    cos, sin = _rope_tables(seq)
    q, k, vt = _qkv_rope(x2d, row(mix_norm_g[0]), wq, wk, wvt, cos, sin, batch, seq)
    lqk = jnp.stack([attn_lq1[0], attn_lk1[0], attn_lq2[0], attn_lk2[0]])
    o = _diff_attn(lqk, attn_subln_g[0].reshape(V_DIM, 1),
                   q.reshape(batch, seq, d), k.reshape(batch, seq, d), vt, lambda_init)
    x2d = _proj_mlp(x2d, o.reshape(batch * seq, d), attn_w_o[0].astype(BF16), row(mlp_norm_g[0]),
                    mlp_w1[0].astype(BF16), mlp_w2[0].astype(BF16), row(final_norm_g), False)

    gated = _rglru(x2d.reshape(batch, seq, d), row(mix_norm_g[1]),
                   rec_w_x[0].astype(BF16), rec_w_y[0].astype(BF16), rec_conv_w[0], row(rec_conv_b[0]),
                   rec_w_a[0].astype(BF16), row(rec_b_a[0]), rec_w_i[0].astype(BF16), row(rec_b_i[0]),
                   row(rec_lambda[0]))
    out = _proj_mlp(x2d, gated.reshape(batch * seq, d), rec_w_o[0].astype(BF16), row(mlp_norm_g[1]),
                    mlp_w1[1].astype(BF16), mlp_w2[1].astype(BF16), row(final_norm_g), True)
    return out.reshape(batch, seq, d)
```

```python
import functools
import math

import jax
import jax.numpy as jnp
from jax import lax
from jax.experimental import pallas as pl
from jax.experimental.pallas import tpu as pltpu

F32 = jnp.float32
BF16 = jnp.bfloat16

D_MODEL = 1024
CHUNK = 64
ATTN_HEADS = 8
HEAD_DIM = 64
V_DIM = 128
VT_ROWS = V_DIM + 16
ROPE_THETA = 10000.0
RG_BLOCK = 256
RG_HEADS = 4
CONV_WIDTH = 4
RG_C = 8.0
D_FF = 4 * D_MODEL
NORM_EPS = 1e-6
SUBLN_EPS = 1e-5

LANES = 128
SUBLANES = 8
VMEM_LIMIT = 56 * 1024 * 1024

TM_QKV = 512
TQ = 512
TK = 512
N_STREAMS = 4
ROW_CHUNK = 64
BF16_ROWS = 2 * SUBLANES
TM_MLP = 512
FF_CHUNK = 1024
TM_REC = 256
REC_STEPS = TM_REC // SUBLANES
REC_STREAMS = 2

NT = (((1,), (1,)), ((), ()))
Q_SCALE = (HEAD_DIM ** -0.5) * math.log2(math.e)


def _rms(x, g, eps):
    ms = jnp.mean(x * x, axis=-1, keepdims=True)
    return (x * lax.rsqrt(ms + eps)) * g


def _qkv_kernel(x_ref, g_ref, wq_ref, wk_ref, wvt_ref, cos_ref, sin_ref, q_ref, k_ref, vt_ref):
    y = _rms(x_ref[...], g_ref[...], NORM_EPS).astype(BF16)
    cos = cos_ref[...]
    sin = sin_ref[...]

    def rope(t):
        return t * cos + pltpu.roll(t, 64, 1) * sin

    q = jnp.dot(y, wq_ref[...], preferred_element_type=F32)
    for h in range(ATTN_HEADS):
        sl = slice(h * LANES, (h + 1) * LANES)
        q_ref[:, sl] = (rope(q[:, sl]) * Q_SCALE).astype(BF16)
    k = jnp.dot(y, wk_ref[...], preferred_element_type=F32)
    for h in range(ATTN_HEADS):
        sl = slice(h * LANES, (h + 1) * LANES)
        k_ref[:, sl] = rope(k[:, sl]).astype(BF16)
    vt = lax.dot_general(wvt_ref[...], y, NT, preferred_element_type=F32)
    for h in range(ATTN_HEADS):
        for t in range(TM_QKV // TK):
            vt_ref[0, h, t, :V_DIM, :] = vt[h * V_DIM:(h + 1) * V_DIM, t * TK:(t + 1) * TK].astype(BF16)
            vt_ref[0, h, t, V_DIM:, :] = jnp.ones((VT_ROWS - V_DIM, TK), BF16)


def _qkv_rope(x2d, g, wq, wk, wvt, cos, sin, batch, seq):
    n_tok = x2d.shape[0]
    tiles_per_seq = seq // TM_QKV
    const = lambda i: (0, 0)
    return pl.pallas_call(
        _qkv_kernel,
        grid=(n_tok // TM_QKV,),
        in_specs=[
            pl.BlockSpec((TM_QKV, D_MODEL), lambda i: (i, 0)),
            pl.BlockSpec((1, D_MODEL), const),
            pl.BlockSpec((D_MODEL, D_MODEL), const),
            pl.BlockSpec((D_MODEL, D_MODEL), const),
            pl.BlockSpec((D_MODEL, D_MODEL), const),
            pl.BlockSpec((TM_QKV, LANES), lambda i: (i % tiles_per_seq, 0)),
            pl.BlockSpec((TM_QKV, LANES), lambda i: (i % tiles_per_seq, 0)),
        ],
        out_specs=[
            pl.BlockSpec((TM_QKV, D_MODEL), lambda i: (i, 0)),
            pl.BlockSpec((TM_QKV, D_MODEL), lambda i: (i, 0)),
            pl.BlockSpec((1, ATTN_HEADS, TM_QKV // TK, VT_ROWS, TK),
                         lambda i: (i // tiles_per_seq, 0, i % tiles_per_seq, 0, 0)),
        ],
        out_shape=[
            jax.ShapeDtypeStruct((n_tok, D_MODEL), BF16),
            jax.ShapeDtypeStruct((n_tok, D_MODEL), BF16),
            jax.ShapeDtypeStruct((batch, ATTN_HEADS, seq // TK, VT_ROWS, TK), BF16),
        ],
        compiler_params=pltpu.CompilerParams(
            dimension_semantics=("arbitrary",), vmem_limit_bytes=VMEM_LIMIT),
        name="qkv_rope",
    )(x2d, g, wq, wk, wvt, cos, sin)


def _attn_kernel(lqk_ref, g_ref, q_ref, k_ref, vt_ref, o_ref, m_ref, l_ref, acc_ref, alpha_ref, p_ref,
                 *, lambda_init):
    qi = pl.program_id(2)

    lane = lax.broadcasted_iota(jnp.int32, (BF16_ROWS, LANES), 1)
    is_map1 = (lane // (HEAD_DIM // 2)) % 2 == 0
    keep1 = jnp.where(is_map1, 1.0, 0.0).astype(BF16)
    keep2 = jnp.where(is_map1, 0.0, 1.0).astype(BF16)
    keep1 = jnp.concatenate([keep1] * (TQ // BF16_ROWS), axis=0)
    keep2 = jnp.concatenate([keep2] * (TQ // BF16_ROWS), axis=0)
    q_st = []
    for s in range(N_STREAMS):
        q = q_ref[0, :, s * LANES:(s + 1) * LANES]
        q_st.append(jnp.concatenate([q * keep1, q * keep2], axis=0))

    m_ref[...] = jnp.full(m_ref.shape, -1e30, F32)
    l_ref[...] = jnp.zeros(l_ref.shape, F32)
    acc_ref[...] = jnp.zeros(acc_ref.shape, F32)

    def value_matmul(s, j):
        return jnp.dot(vt_ref[0, s, j], p_ref[s], preferred_element_type=F32)

    def accumulate(s, pv):
        acc_ref[s] = alpha_ref[s] * acc_ref[s] + pv[:V_DIM, :]
        l_ref[s] = alpha_ref[s] * l_ref[s] + pv[V_DIM:V_DIM + 1, :]

    def score_softmax(j, allowed, pending):
        rows = pl.ds(pl.multiple_of(j * TK, TK), TK)
        scores, pv = [], {}
        for s in range(N_STREAMS):
            scores.append(lax.dot_general(k_ref[0, rows, s * LANES:(s + 1) * LANES], q_st[s], NT,
                                          preferred_element_type=F32))
            if pending is not None and s >= 1:
                pv[s - 1] = value_matmul(s - 1, pending)
        if pending is not None:
            pv[N_STREAMS - 1] = value_matmul(N_STREAMS - 1, pending)

        masked = {}

        def groups(s, c):
            for g in range(c * ROW_CHUNK, (c + 1) * ROW_CHUNK, BF16_ROWS):
                if (s, g) not in masked:
                    blk = scores[s][g:g + BF16_ROWS, :]
                    if allowed is not None:
                        blk = jnp.where(allowed[g:g + BF16_ROWS, :], blk, -jnp.inf)
                    masked[s, g] = blk
                yield g, masked[s, g]

        m_new = {}

        def max_pass(s):
            if pending is not None:
                accumulate(s, pv[s])
            mx = None
            for c in range(TK // ROW_CHUNK):
                for _, blk in groups(s, c):
                    mx = blk if mx is None else jnp.maximum(mx, blk)
                yield
            m_old = m_ref[s]
            m_new[s] = jnp.maximum(m_old, jnp.max(mx, axis=0, keepdims=True))
            alpha_ref[s] = jnp.exp2(m_old - m_new[s])
            m_ref[s] = m_new[s]

        def exp_pass(s):
            for c in range(TK // ROW_CHUNK):
                for g, blk in groups(s, c):
                    p_ref[s, g:g + BF16_ROWS, :] = jnp.exp2(blk - m_new[s]).astype(BF16)
                yield

        for _ in max_pass(0):
            pass
        for s in range(N_STREAMS):
            ahead = max_pass(s + 1) if s + 1 < N_STREAMS else iter(())
            for _ in exp_pass(s):
                next(ahead, None)
            for _ in ahead:
                pass

    q_per_k = TK // TQ
    last = qi // q_per_k
    key_chunk = lax.broadcasted_iota(jnp.int32, (TK, 2 * TQ), 0) // CHUNK
    qry_chunk = (lax.broadcasted_iota(jnp.int32, (TK, 2 * TQ), 1) % TQ) // CHUNK
    qry_offset = (qi % q_per_k) * (TQ // CHUNK)
    score_softmax(last, key_chunk - qry_chunk <= qry_offset, None)

    def body(j, pending):
        score_softmax(j, None, pending)
        return j

    pending = lax.fori_loop(0, last, body, last)
    for s in range(N_STREAMS):
        accumulate(s, value_matmul(s, pending))

    lqk = lqk_ref[...]
    lam = (jnp.exp(jnp.sum(lqk[0:1] * lqk[1:2], axis=1, keepdims=True))
           - jnp.exp(jnp.sum(lqk[2:3] * lqk[3:4], axis=1, keepdims=True))
           + lambda_init)
    for s in range(N_STREAMS):
        acc = acc_ref[s]
        l = l_ref[s]
        o = acc[:, :TQ] / l[:, :TQ] - lam * (acc[:, TQ:] / l[:, TQ:])
        ms = jnp.mean(o * o, axis=0, keepdims=True)
        y = ((o * lax.rsqrt(ms + SUBLN_EPS)) * g_ref[...]) * (1.0 - lambda_init)
        o_ref[0, :, s * LANES:(s + 1) * LANES] = y.T.astype(BF16)


def _diff_attn(lqk, g_col, q, k, vt, lambda_init):
    batch, seq, _ = q.shape
    width = N_STREAMS * LANES
    kernel = functools.partial(_attn_kernel, lambda_init=lambda_init)
    return pl.pallas_call(
        kernel,
        grid=(batch, ATTN_HEADS // N_STREAMS, seq // TQ),
        in_specs=[
            pl.BlockSpec((4, HEAD_DIM), lambda b, h, i: (0, 0)),
            pl.BlockSpec((V_DIM, 1), lambda b, h, i: (0, 0)),
            pl.BlockSpec((1, TQ, width), lambda b, h, i: (b, i, h)),
            pl.BlockSpec((1, seq, width), lambda b, h, i: (b, 0, h)),
            pl.BlockSpec((1, N_STREAMS, seq // TK, VT_ROWS, TK), lambda b, h, i: (b, h, 0, 0, 0)),
        ],
        out_specs=pl.BlockSpec((1, TQ, width), lambda b, h, i: (b, i, h)),
        out_shape=jax.ShapeDtypeStruct((batch, seq, ATTN_HEADS * V_DIM), BF16),
        scratch_shapes=[
            pltpu.VMEM((N_STREAMS, 1, 2 * TQ), F32),
            pltpu.VMEM((N_STREAMS, 1, 2 * TQ), F32),
            pltpu.VMEM((N_STREAMS, V_DIM, 2 * TQ), F32),
            pltpu.VMEM((N_STREAMS, 1, 2 * TQ), F32),
            pltpu.VMEM((N_STREAMS, TK, 2 * TQ), BF16),
        ],
        compiler_params=pltpu.CompilerParams(
            dimension_semantics=("arbitrary", "arbitrary", "arbitrary"),
            vmem_limit_bytes=VMEM_LIMIT),
        name="diff_attn",
    )(lqk, g_col, q, k, vt)


def _proj_mlp_kernel(x_ref, a_ref, wo_ref, g_ref, w1_ref, w2_ref, gf_ref, out_ref, *, final_norm):
    x1 = x_ref[...] + jnp.dot(a_ref[...], wo_ref[...], preferred_element_type=F32)
    h = _rms(x1, g_ref[...], NORM_EPS).astype(BF16)
    acc = x1
    for c in range(D_FF // FF_CHUNK):
        sl = slice(c * FF_CHUNK, (c + 1) * FF_CHUNK)
        u = jnp.dot(h, w1_ref[:, sl], preferred_element_type=F32)
        u = jnp.square(jnp.maximum(u, 0.0)).astype(BF16)
        acc = acc + jnp.dot(u, w2_ref[sl, :], preferred_element_type=F32)
    if final_norm:
        acc = _rms(acc, gf_ref[...], NORM_EPS)
    out_ref[...] = acc


def _proj_mlp(x2d, a2d, wo, g, w1, w2, gf, final_norm):
    n_tok = x2d.shape[0]
    const = lambda i: (0, 0)
    resident = pl.Buffered(1)
    kernel = functools.partial(_proj_mlp_kernel, final_norm=final_norm)
    return pl.pallas_call(
        kernel,
        grid=(n_tok // TM_MLP,),
        in_specs=[
            pl.BlockSpec((TM_MLP, D_MODEL), lambda i: (i, 0)),
            pl.BlockSpec((TM_MLP, D_MODEL), lambda i: (i, 0)),
            pl.BlockSpec((D_MODEL, D_MODEL), const, pipeline_mode=resident),
            pl.BlockSpec((1, D_MODEL), const),
            pl.BlockSpec((D_MODEL, D_FF), const, pipeline_mode=resident),
            pl.BlockSpec((D_FF, D_MODEL), const, pipeline_mode=resident),
            pl.BlockSpec((1, D_MODEL), const),
        ],
        out_specs=pl.BlockSpec((TM_MLP, D_MODEL), lambda i: (i, 0)),
        out_shape=jax.ShapeDtypeStruct((n_tok, D_MODEL), F32),
        compiler_params=pltpu.CompilerParams(
            dimension_semantics=("arbitrary",), vmem_limit_bytes=VMEM_LIMIT),
        name="proj_mlp_final" if final_norm else "proj_mlp",
    )(x2d, a2d, wo, g, w1, w2, gf)


def _rglru_kernel(x_ref, g_ref, perm_ref, unperm_ref, wx_ref, wy_ref, cw_ref, cb_ref, wa_ref, ba_ref,
                  wi_ref, bi_ref, lam_ref, out_ref, tail_ref, h_ref):
    t = pl.program_id(1)
    n_tail = (CONV_WIDTH - 1) * SUBLANES

    @pl.when(t == 0)
    def _():
        tail_ref[...] = jnp.zeros(tail_ref.shape, F32)
        h_ref[...] = jnp.zeros(h_ref.shape, F32)

    group = lambda v, r: v[r * SUBLANES:(r + 1) * SUBLANES, :]
    sub = lax.broadcasted_iota(jnp.int32, (SUBLANES, D_MODEL), 0)

    def project(s):
        h = _rms(x_ref[s], g_ref[...], NORM_EPS).astype(BF16)
        h = jnp.dot(perm_ref[...], h, preferred_element_type=F32).astype(BF16)
        gate = jax.nn.gelu(jnp.dot(h, wy_ref[...], preferred_element_type=F32))
        xp = jnp.dot(h, wx_ref[...], preferred_element_type=F32)
        return gate, xp

    def conv_and_gates(s, xp):
        lead = []
        for k in range(CONV_WIDTH - 1):
            cur = group(xp, REC_STEPS - (CONV_WIDTH - 1) + k)
            prev_last = tail_ref[s, (k + 1) * SUBLANES - 1:(k + 1) * SUBLANES, :]
            lead.append(jnp.where(sub == 0, prev_last, pltpu.roll(cur, 1, 0)))
        tail_ref[s] = xp[TM_REC - n_tail:, :]
        ext = jnp.concatenate(lead + [xp], axis=0)
        cw = cw_ref[...]
        xb = cb_ref[...]
        for j in range(CONV_WIDTH):
            xb = xb + ext[j * SUBLANES:j * SUBLANES + TM_REC, :] * cw[j:j + 1, :]
        xb16 = xb.astype(BF16)

        def block_diag(w_ref):
            return jnp.concatenate(
                [jnp.dot(xb16[:, n * RG_BLOCK:(n + 1) * RG_BLOCK], w_ref[n],
                         preferred_element_type=F32) for n in range(RG_HEADS)], axis=1)

        return xb, block_diag(wa_ref), block_diag(wi_ref)

    def recur(s, gate, xb, ra, ri):
        r = jax.nn.sigmoid(ra + ba_ref[...])
        i = jax.nn.sigmoid(ri + bi_ref[...])
        neg_lam = -lam_ref[...]
        softplus = jnp.maximum(neg_lam, 0.0) + jnp.log1p(jnp.exp(-jnp.abs(neg_lam)))
        log_a = (-RG_C * r) * softplus
        a = jnp.exp(log_a)
        u = jnp.sqrt(1.0 - a * a) * (i * xb)
        hz, pz = [group(u, 0)], [group(a, 0)]
        for k in range(1, REC_STEPS):
            a_k = group(a, k)
            hz.append(a_k * hz[-1] + group(u, k))
            pz.append(a_k * pz[-1])
        start = h_ref[s]
        starts = []
        for b in range(SUBLANES):
            starts.append(start)
            start = hz[-1][b:b + 1, :] + pz[-1][b:b + 1, :] * start
        h_ref[s] = start
        starts = jnp.concatenate(starts, axis=0)
        y = jnp.concatenate([(hz[k] + pz[k] * starts) * group(gate, k) for k in range(REC_STEPS)],
                            axis=0).astype(BF16)
        out_ref[s] = jnp.dot(unperm_ref[...], y, preferred_element_type=F32).astype(BF16)

    streams = range(REC_STREAMS)
    proj = [project(s) for s in streams]
    mid = [conv_and_gates(s, proj[s][1]) for s in streams]
    for s in streams:
        recur(s, proj[s][0], *mid[s])


def _rglru(x, g, wx, wy, cw, cb, wa, ba, wi, bi, lam):
    batch, seq, _ = x.shape
    src = (jnp.arange(TM_REC) % SUBLANES) * REC_STEPS + jnp.arange(TM_REC) // SUBLANES
    perm = (src[:, None] == jnp.arange(TM_REC)[None, :]).astype(BF16)
    c2 = lambda b, t: (0, 0)
    c3 = lambda b, t: (0, 0, 0)
    return pl.pallas_call(
        _rglru_kernel,
        grid=(batch // REC_STREAMS, seq // TM_REC),
        in_specs=[
            pl.BlockSpec((REC_STREAMS, TM_REC, D_MODEL), lambda b, t: (b, t, 0)),
            pl.BlockSpec((1, D_MODEL), c2),
            pl.BlockSpec((TM_REC, TM_REC), c2),
            pl.BlockSpec((TM_REC, TM_REC), c2),
            pl.BlockSpec((D_MODEL, D_MODEL), c2),
            pl.BlockSpec((D_MODEL, D_MODEL), c2),
            pl.BlockSpec((CONV_WIDTH, D_MODEL), c2),
            pl.BlockSpec((1, D_MODEL), c2),
            pl.BlockSpec((RG_HEADS, RG_BLOCK, RG_BLOCK), c3),
            pl.BlockSpec((1, D_MODEL), c2),
            pl.BlockSpec((RG_HEADS, RG_BLOCK, RG_BLOCK), c3),
            pl.BlockSpec((1, D_MODEL), c2),
            pl.BlockSpec((1, D_MODEL), c2),
        ],
        out_specs=pl.BlockSpec((REC_STREAMS, TM_REC, D_MODEL), lambda b, t: (b, t, 0)),
        out_shape=jax.ShapeDtypeStruct((batch, seq, D_MODEL), BF16),
        scratch_shapes=[
            pltpu.VMEM((REC_STREAMS, (CONV_WIDTH - 1) * SUBLANES, D_MODEL), F32),
            pltpu.VMEM((REC_STREAMS, 1, D_MODEL), F32),
        ],
        compiler_params=pltpu.CompilerParams(
            dimension_semantics=("arbitrary", "arbitrary"), vmem_limit_bytes=VMEM_LIMIT),
        name="rglru",
    )(x, g, perm, perm.T, wx, wy, cw, cb, wa, ba, wi, bi, lam)


def _rope_tables(seq):
    half = HEAD_DIM // 2
    inv_freq = 1.0 / (ROPE_THETA ** (jnp.arange(0, HEAD_DIM, 2, dtype=F32) / HEAD_DIM))
    ang = jnp.arange(seq, dtype=jnp.int32).astype(F32)[:, None] * inv_freq[None, :]
    cos = jnp.tile(jnp.cos(ang), (1, LANES // half))
    sin = jnp.tile(jnp.sin(ang), (1, LANES // half))
    sign = jnp.where(jnp.arange(LANES) < LANES // 2, -1.0, 1.0).astype(F32)
    return cos, sin * sign[None, :]


def _permute_heads(w):
    d_in = w.shape[0]
    half = HEAD_DIM // 2
    return w.reshape(d_in, ATTN_HEADS, 2, 2, half).transpose(0, 1, 3, 2, 4).reshape(d_in, -1)


def kernel(x, mix_norm_g, mlp_norm_g, attn_w_qkv, attn_w_o, attn_lq1, attn_lk1, attn_lq2, attn_lk2,
           attn_subln_g, rec_w_x, rec_w_y, rec_conv_w, rec_conv_b, rec_w_a, rec_b_a, rec_w_i, rec_b_i,
           rec_lambda, rec_w_o, mlp_w1, mlp_w2, final_norm_g):
    batch, seq, d = x.shape
    x2d = x.reshape(batch * seq, d)
    row = lambda v: v.reshape(1, -1)

    lambda_init = 0.8 - 0.6 * math.exp(-0.3 * 0)
    w_qkv = attn_w_qkv[0]
    wq = _permute_heads(w_qkv[:, :d]).astype(BF16)
    wk = _permute_heads(w_qkv[:, d:2 * d]).astype(BF16)
    wvt = w_qkv[:, 2 * d:].T.astype(BF16)
    cos, sin = _rope_tables(seq)
    q, k, vt = _qkv_rope(x2d, row(mix_norm_g[0]), wq, wk, wvt, cos, sin, batch, seq)
    lqk = jnp.stack([attn_lq1[0], attn_lk1[0], attn_lq2[0], attn_lk2[0]])
    o = _diff_attn(lqk, attn_subln_g[0].reshape(V_DIM, 1),
                   q.reshape(batch, seq, d), k.reshape(batch, seq, d), vt, lambda_init)
    x2d = _proj_mlp(x2d, o.reshape(batch * seq, d), attn_w_o[0].astype(BF16), row(mlp_norm_g[0]),
                    mlp_w1[0].astype(BF16), mlp_w2[0].astype(BF16), row(final_norm_g), False)

    gated = _rglru(x2d.reshape(batch, seq, d), row(mix_norm_g[1]),
                   rec_w_x[0].astype(BF16), rec_w_y[0].astype(BF16), rec_conv_w[0], row(rec_conv_b[0]),
                   rec_w_a[0].astype(BF16), row(rec_b_a[0]), rec_w_i[0].astype(BF16), row(rec_b_i[0]),
                   row(rec_lambda[0]))
    out = _proj_mlp(x2d, gated.reshape(batch * seq, d), rec_w_o[0].astype(BF16), row(mlp_norm_g[1]),
                    mlp_w1[1].astype(BF16), mlp_w2[1].astype(BF16), row(final_norm_g), True)
    return out.reshape(batch, seq, d)
```

```python
import functools
import math

import jax
import jax.numpy as jnp
from jax import lax
from jax.experimental import pallas as pl
from jax.experimental.pallas import tpu as pltpu

F32 = jnp.float32
BF16 = jnp.bfloat16

D_MODEL = 1024
CHUNK = 64
ATTN_HEADS = 8
HEAD_DIM = 64
V_DIM = 128
VT_ROWS = V_DIM + 16
ROPE_THETA = 10000.0
RG_BLOCK = 256
RG_HEADS = 4
CONV_WIDTH = 4
RG_C = 8.0
D_FF = 4 * D_MODEL
NORM_EPS = 1e-6
SUBLN_EPS = 1e-5

LANES = 128
SUBLANES = 8
VMEM_LIMIT = 56 * 1024 * 1024

TM_QKV = 512
TQ = 512
TK = TQ
N_STREAMS = 4
ROW_CHUNK = 64
BF16_ROWS = 2 * SUBLANES
TM_MLP = 512
FF_CHUNK = 1024
TM_REC = 256
REC_STEPS = TM_REC // SUBLANES
REC_STREAMS = 4

NT = (((1,), (1,)), ((), ()))
Q_SCALE = (HEAD_DIM ** -0.5) * math.log2(math.e)


def _rms(x, g, eps):
    ms = jnp.mean(x * x, axis=-1, keepdims=True)
    return (x * lax.rsqrt(ms + eps)) * g


def _qkv_kernel(x_ref, g_ref, wq_ref, wk_ref, wvt_ref, cos_ref, sin_ref, q_ref, k_ref, vt_ref):
    y = _rms(x_ref[...], g_ref[...], NORM_EPS).astype(BF16)
    cos = cos_ref[...]
    sin = sin_ref[...]

    def rope(t):
        return t * cos + pltpu.roll(t, 64, 1) * sin

    q = jnp.dot(y, wq_ref[...], preferred_element_type=F32)
    for h in range(ATTN_HEADS):
        sl = slice(h * LANES, (h + 1) * LANES)
        q_ref[:, sl] = (rope(q[:, sl]) * Q_SCALE).astype(BF16)
    k = jnp.dot(y, wk_ref[...], preferred_element_type=F32)
    for h in range(ATTN_HEADS):
        sl = slice(h * LANES, (h + 1) * LANES)
        k_ref[:, sl] = rope(k[:, sl]).astype(BF16)
    vt = lax.dot_general(wvt_ref[...], y, NT, preferred_element_type=F32)
    for h in range(ATTN_HEADS):
        for t in range(TM_QKV // TK):
            vt_ref[0, h, t, :V_DIM, :] = vt[h * V_DIM:(h + 1) * V_DIM, t * TK:(t + 1) * TK].astype(BF16)
            vt_ref[0, h, t, V_DIM:, :] = jnp.ones((VT_ROWS - V_DIM, TK), BF16)


def _qkv_rope(x2d, g, wq, wk, wvt, cos, sin, batch, seq):
    n_tok = x2d.shape[0]
    tiles_per_seq = seq // TM_QKV
    const = lambda i: (0, 0)
    return pl.pallas_call(
        _qkv_kernel,
        grid=(n_tok // TM_QKV,),
        in_specs=[
            pl.BlockSpec((TM_QKV, D_MODEL), lambda i: (i, 0)),
            pl.BlockSpec((1, D_MODEL), const),
            pl.BlockSpec((D_MODEL, D_MODEL), const),
            pl.BlockSpec((D_MODEL, D_MODEL), const),
            pl.BlockSpec((D_MODEL, D_MODEL), const),
            pl.BlockSpec((TM_QKV, LANES), lambda i: (i % tiles_per_seq, 0)),
            pl.BlockSpec((TM_QKV, LANES), lambda i: (i % tiles_per_seq, 0)),
        ],
        out_specs=[
            pl.BlockSpec((TM_QKV, D_MODEL), lambda i: (i, 0)),
            pl.BlockSpec((TM_QKV, D_MODEL), lambda i: (i, 0)),
            pl.BlockSpec((1, ATTN_HEADS, TM_QKV // TK, VT_ROWS, TK),
                         lambda i: (i // tiles_per_seq, 0, i % tiles_per_seq, 0, 0)),
        ],
        out_shape=[
            jax.ShapeDtypeStruct((n_tok, D_MODEL), BF16),
            jax.ShapeDtypeStruct((n_tok, D_MODEL), BF16),
            jax.ShapeDtypeStruct((batch, ATTN_HEADS, seq // TK, VT_ROWS, TK), BF16),
        ],
        compiler_params=pltpu.CompilerParams(
            dimension_semantics=("arbitrary",), vmem_limit_bytes=VMEM_LIMIT),
        name="qkv_rope",
    )(x2d, g, wq, wk, wvt, cos, sin)


def _attn_kernel(lqk_ref, g_ref, q_ref, k_ref, vt_ref, o_ref, m_ref, l_ref, acc_ref, alpha_ref, p_ref,
                 *, lambda_init):
    qi = pl.program_id(2)

    lane = lax.broadcasted_iota(jnp.int32, (BF16_ROWS, LANES), 1)
    is_map1 = (lane // (HEAD_DIM // 2)) % 2 == 0
    keep1 = jnp.where(is_map1, 1.0, 0.0).astype(BF16)
    keep2 = jnp.where(is_map1, 0.0, 1.0).astype(BF16)
    keep1 = jnp.concatenate([keep1] * (TQ // BF16_ROWS), axis=0)
    keep2 = jnp.concatenate([keep2] * (TQ // BF16_ROWS), axis=0)
    q_st = []
    for s in range(N_STREAMS):
        q = q_ref[0, :, s * LANES:(s + 1) * LANES]
        q_st.append(jnp.concatenate([q * keep1, q * keep2], axis=0))

    m_ref[...] = jnp.full(m_ref.shape, -1e30, F32)
    l_ref[...] = jnp.zeros(l_ref.shape, F32)
    acc_ref[...] = jnp.zeros(acc_ref.shape, F32)

    def value_matmul(s, j):
        return jnp.dot(vt_ref[0, s, j], p_ref[s], preferred_element_type=F32)

    def accumulate(s, pv):
        acc_ref[s] = alpha_ref[s] * acc_ref[s] + pv[:V_DIM, :]
        l_ref[s] = alpha_ref[s] * l_ref[s] + pv[V_DIM:V_DIM + 1, :]

    upper_half = lax.broadcasted_iota(jnp.int32, (BF16_ROWS, LANES), 1) >= CHUNK

    def score_softmax(j, own, pending):
        rows = pl.ds(pl.multiple_of(j * TK, TK), TK)
        scores, pv = [], {}
        for s in range(N_STREAMS):
            scores.append(lax.dot_general(k_ref[0, rows, s * LANES:(s + 1) * LANES], q_st[s], NT,
                                          preferred_element_type=F32))
            if pending is not None and s >= 1:
                pv[s - 1] = value_matmul(s - 1, pending)
        if pending is not None:
            pv[N_STREAMS - 1] = value_matmul(N_STREAMS - 1, pending)

        width = LANES if own else 2 * TQ
        lane_blocks = [slice(c, c + width) for c in range(0, 2 * TQ, width)]
        cache = {}

        def block(s, g, cols):
            if (s, g, cols.start) not in cache:
                blk = scores[s][g:g + BF16_ROWS, cols]
                if own:
                    key_chunk = g // CHUNK
                    qry_chunk = (cols.start % TQ) // CHUNK
                    if qry_chunk + 1 < key_chunk:
                        blk = None
                    elif qry_chunk < key_chunk:
                        blk = jnp.where(upper_half, blk, -jnp.inf)
                cache[s, g, cols.start] = blk
            return cache[s, g, cols.start]

        def row_groups(c):
            return range(c * ROW_CHUNK, (c + 1) * ROW_CHUNK, BF16_ROWS)

        m_new = {}

        def max_pass(s):
            if pending is not None:
                accumulate(s, pv[s])
            mx = {}
            for c in range(TK // ROW_CHUNK):
                for g in row_groups(c):
                    for cols in lane_blocks:
                        blk = block(s, g, cols)
                        if blk is not None:
                            seen = mx.get(cols.start)
                            mx[cols.start] = blk if seen is None else jnp.maximum(seen, blk)
                yield
            m_old = m_ref[s]
            col_max = jnp.concatenate(
                [jnp.max(mx[cols.start], axis=0, keepdims=True) for cols in lane_blocks], axis=1)
            m_new[s] = jnp.maximum(m_old, col_max)
            alpha_ref[s] = jnp.exp2(m_old - m_new[s])
            m_ref[s] = m_new[s]

        def exp_pass(s):
            for c in range(TK // ROW_CHUNK):
                for g in row_groups(c):
                    for cols in lane_blocks:
                        blk = block(s, g, cols)
                        if blk is None:
                            p = jnp.zeros((BF16_ROWS, width), BF16)
                        else:
                            p = jnp.exp2(blk - m_new[s][:, cols]).astype(BF16)
                        p_ref[s, g:g + BF16_ROWS, cols] = p
                yield

        for _ in max_pass(0):
            pass
        for s in range(N_STREAMS):
            ahead = max_pass(s + 1) if s + 1 < N_STREAMS else iter(())
            for _ in exp_pass(s):
                next(ahead, None)
            for _ in ahead:
                pass

    score_softmax(qi, True, None)

    def body(j, pending):
        score_softmax(j, False, pending)
        return j

    pending = lax.fori_loop(0, qi, body, qi)
    for s in range(N_STREAMS):
        accumulate(s, value_matmul(s, pending))

    lqk = lqk_ref[...]
    lam = (jnp.exp(jnp.sum(lqk[0:1] * lqk[1:2], axis=1, keepdims=True))
           - jnp.exp(jnp.sum(lqk[2:3] * lqk[3:4], axis=1, keepdims=True))
           + lambda_init)
    for s in range(N_STREAMS):
        acc = acc_ref[s]
        l = l_ref[s]
        o = acc[:, :TQ] / l[:, :TQ] - lam * (acc[:, TQ:] / l[:, TQ:])
        ms = jnp.mean(o * o, axis=0, keepdims=True)
        y = ((o * lax.rsqrt(ms + SUBLN_EPS)) * g_ref[...]) * (1.0 - lambda_init)
        o_ref[0, :, s * LANES:(s + 1) * LANES] = y.T.astype(BF16)


def _diff_attn(lqk, g_col, q, k, vt, lambda_init):
    batch, seq, _ = q.shape
    width = N_STREAMS * LANES
    kernel = functools.partial(_attn_kernel, lambda_init=lambda_init)
    return pl.pallas_call(
        kernel,
        grid=(batch, ATTN_HEADS // N_STREAMS, seq // TQ),
        in_specs=[
            pl.BlockSpec((4, HEAD_DIM), lambda b, h, i: (0, 0)),
            pl.BlockSpec((V_DIM, 1), lambda b, h, i: (0, 0)),
            pl.BlockSpec((1, TQ, width), lambda b, h, i: (b, i, h)),
            pl.BlockSpec((1, seq, width), lambda b, h, i: (b, 0, h)),
            pl.BlockSpec((1, N_STREAMS, seq // TK, VT_ROWS, TK), lambda b, h, i: (b, h, 0, 0, 0)),
        ],
        out_specs=pl.BlockSpec((1, TQ, width), lambda b, h, i: (b, i, h)),
        out_shape=jax.ShapeDtypeStruct((batch, seq, ATTN_HEADS * V_DIM), BF16),
        scratch_shapes=[
            pltpu.VMEM((N_STREAMS, 1, 2 * TQ), F32),
            pltpu.VMEM((N_STREAMS, 1, 2 * TQ), F32),
            pltpu.VMEM((N_STREAMS, V_DIM, 2 * TQ), F32),
            pltpu.VMEM((N_STREAMS, 1, 2 * TQ), F32),
            pltpu.VMEM((N_STREAMS, TK, 2 * TQ), BF16),
        ],
        compiler_params=pltpu.CompilerParams(
            dimension_semantics=("arbitrary", "arbitrary", "arbitrary"),
            vmem_limit_bytes=VMEM_LIMIT),
        name="diff_attn",
    )(lqk, g_col, q, k, vt)


def _proj_mlp_kernel(x_ref, a_ref, wo_ref, g_ref, w1_ref, w2_ref, gf_ref, out_ref, *, final_norm):
    x1 = x_ref[...] + jnp.dot(a_ref[...], wo_ref[...], preferred_element_type=F32)
    h = _rms(x1, g_ref[...], NORM_EPS).astype(BF16)
    acc = x1
    for c in range(D_FF // FF_CHUNK):
        sl = slice(c * FF_CHUNK, (c + 1) * FF_CHUNK)
        u = jnp.dot(h, w1_ref[:, sl], preferred_element_type=F32)
        u = jnp.square(jnp.maximum(u, 0.0)).astype(BF16)
        acc = acc + jnp.dot(u, w2_ref[sl, :], preferred_element_type=F32)
    if final_norm:
        acc = _rms(acc, gf_ref[...], NORM_EPS)
    out_ref[...] = acc


def _proj_mlp(x2d, a2d, wo, g, w1, w2, gf, final_norm):
    n_tok = x2d.shape[0]
    const = lambda i: (0, 0)
    resident = pl.Buffered(1)
    kernel = functools.partial(_proj_mlp_kernel, final_norm=final_norm)
    return pl.pallas_call(
        kernel,
        grid=(n_tok // TM_MLP,),
        in_specs=[
            pl.BlockSpec((TM_MLP, D_MODEL), lambda i: (i, 0)),
            pl.BlockSpec((TM_MLP, D_MODEL), lambda i: (i, 0)),
            pl.BlockSpec((D_MODEL, D_MODEL), const, pipeline_mode=resident),
            pl.BlockSpec((1, D_MODEL), const),
            pl.BlockSpec((D_MODEL, D_FF), const, pipeline_mode=resident),
            pl.BlockSpec((D_FF, D_MODEL), const, pipeline_mode=resident),
            pl.BlockSpec((1, D_MODEL), const),
        ],
        out_specs=pl.BlockSpec((TM_MLP, D_MODEL), lambda i: (i, 0)),
        out_shape=jax.ShapeDtypeStruct((n_tok, D_MODEL), F32),
        compiler_params=pltpu.CompilerParams(
            dimension_semantics=("arbitrary",), vmem_limit_bytes=VMEM_LIMIT),
        name="proj_mlp_final" if final_norm else "proj_mlp",
    )(x2d, a2d, wo, g, w1, w2, gf)


def _rglru_kernel(x_ref, g_ref, perm_ref, unperm_ref, wx_ref, wy_ref, cw_ref, cb_ref, wa_ref, ba_ref,
                  wi_ref, bi_ref, lam_ref, out_ref, tail_ref, h_ref):
    t = pl.program_id(1)
    n_tail = (CONV_WIDTH - 1) * SUBLANES

    @pl.when(t == 0)
    def _():
        tail_ref[...] = jnp.zeros(tail_ref.shape, F32)
        h_ref[...] = jnp.zeros(h_ref.shape, F32)

    group = lambda v, r: v[r * SUBLANES:(r + 1) * SUBLANES, :]
    sub = lax.broadcasted_iota(jnp.int32, (SUBLANES, D_MODEL), 0)

    def project(s):
        h = _rms(x_ref[s], g_ref[...], NORM_EPS).astype(BF16)
        h = jnp.dot(perm_ref[...], h, preferred_element_type=F32).astype(BF16)
        gate = jax.nn.gelu(jnp.dot(h, wy_ref[...], preferred_element_type=F32))
        xp = jnp.dot(h, wx_ref[...], preferred_element_type=F32)
        return gate, xp

    def conv_and_gates(s, xp):
        lead = []
        for k in range(CONV_WIDTH - 1):
            cur = group(xp, REC_STEPS - (CONV_WIDTH - 1) + k)
            prev_last = tail_ref[s, (k + 1) * SUBLANES - 1:(k + 1) * SUBLANES, :]
            lead.append(jnp.where(sub == 0, prev_last, pltpu.roll(cur, 1, 0)))
        tail_ref[s] = xp[TM_REC - n_tail:, :]
        ext = jnp.concatenate(lead + [xp], axis=0)
        cw = cw_ref[...]
        xb = cb_ref[...]
        for j in range(CONV_WIDTH):
            xb = xb + ext[j * SUBLANES:j * SUBLANES + TM_REC, :] * cw[j:j + 1, :]
        xb16 = xb.astype(BF16)

        def block_diag(w_ref):
            return jnp.concatenate(
                [jnp.dot(xb16[:, n * RG_BLOCK:(n + 1) * RG_BLOCK], w_ref[n],
                         preferred_element_type=F32) for n in range(RG_HEADS)], axis=1)

        return xb, block_diag(wa_ref), block_diag(wi_ref)

    def recur(s, gate, xb, ra, ri):
        r = jax.nn.sigmoid(ra + ba_ref[...])
        i = jax.nn.sigmoid(ri + bi_ref[...])
        neg_lam = -lam_ref[...]
        softplus = jnp.maximum(neg_lam, 0.0) + jnp.log1p(jnp.exp(-jnp.abs(neg_lam)))
        log_a = (-RG_C * r) * softplus
        a = jnp.exp(log_a)
        u = jnp.sqrt(1.0 - a * a) * (i * xb)
        hz, pz = [group(u, 0)], [group(a, 0)]
        for k in range(1, REC_STEPS):
            a_k = group(a, k)
            hz.append(a_k * hz[-1] + group(u, k))
            pz.append(a_k * pz[-1])
        start = h_ref[s]
        starts = []
        for b in range(SUBLANES):
            starts.append(start)
            start = hz[-1][b:b + 1, :] + pz[-1][b:b + 1, :] * start
        h_ref[s] = start
        starts = jnp.concatenate(starts, axis=0)
        y = jnp.concatenate([(hz[k] + pz[k] * starts) * group(gate, k) for k in range(REC_STEPS)],
                            axis=0).astype(BF16)
        out_ref[s] = jnp.dot(unperm_ref[...], y, preferred_element_type=F32).astype(BF16)

    streams = range(REC_STREAMS)
    proj = [project(s) for s in streams]
    mid = [conv_and_gates(s, proj[s][1]) for s in streams]
    for s in streams:
        recur(s, proj[s][0], *mid[s])


def _rglru(x, g, wx, wy, cw, cb, wa, ba, wi, bi, lam):
    batch, seq, _ = x.shape
    src = (jnp.arange(TM_REC) % SUBLANES) * REC_STEPS + jnp.arange(TM_REC) // SUBLANES
    perm = (src[:, None] == jnp.arange(TM_REC)[None, :]).astype(BF16)
    c2 = lambda b, t: (0, 0)
    c3 = lambda b, t: (0, 0, 0)
    return pl.pallas_call(
        _rglru_kernel,
        grid=(batch // REC_STREAMS, seq // TM_REC),
        in_specs=[
            pl.BlockSpec((REC_STREAMS, TM_REC, D_MODEL), lambda b, t: (b, t, 0)),
            pl.BlockSpec((1, D_MODEL), c2),
            pl.BlockSpec((TM_REC, TM_REC), c2),
            pl.BlockSpec((TM_REC, TM_REC), c2),
            pl.BlockSpec((D_MODEL, D_MODEL), c2),
            pl.BlockSpec((D_MODEL, D_MODEL), c2),
            pl.BlockSpec((CONV_WIDTH, D_MODEL), c2),
            pl.BlockSpec((1, D_MODEL), c2),
            pl.BlockSpec((RG_HEADS, RG_BLOCK, RG_BLOCK), c3),
            pl.BlockSpec((1, D_MODEL), c2),
            pl.BlockSpec((RG_HEADS, RG_BLOCK, RG_BLOCK), c3),
            pl.BlockSpec((1, D_MODEL), c2),
            pl.BlockSpec((1, D_MODEL), c2),
        ],
        out_specs=pl.BlockSpec((REC_STREAMS, TM_REC, D_MODEL), lambda b, t: (b, t, 0)),
        out_shape=jax.ShapeDtypeStruct((batch, seq, D_MODEL), BF16),
        scratch_shapes=[
            pltpu.VMEM((REC_STREAMS, (CONV_WIDTH - 1) * SUBLANES, D_MODEL), F32),
            pltpu.VMEM((REC_STREAMS, 1, D_MODEL), F32),
        ],
        compiler_params=pltpu.CompilerParams(
            dimension_semantics=("arbitrary", "arbitrary"), vmem_limit_bytes=VMEM_LIMIT),
        name="rglru",
    )(x, g, perm, perm.T, wx, wy, cw, cb, wa, ba, wi, bi, lam)


def _rope_tables(seq):
    half = HEAD_DIM // 2
    inv_freq = 1.0 / (ROPE_THETA ** (jnp.arange(0, HEAD_DIM, 2, dtype=F32) / HEAD_DIM))
    ang = jnp.arange(seq, dtype=jnp.int32).astype(F32)[:, None] * inv_freq[None, :]
    cos = jnp.tile(jnp.cos(ang), (1, LANES // half))
    sin = jnp.tile(jnp.sin(ang), (1, LANES // half))
    sign = jnp.where(jnp.arange(LANES) < LANES // 2, -1.0, 1.0).astype(F32)
    return cos, sin * sign[None, :]


def _permute_heads(w):
    d_in = w.shape[0]
    half = HEAD_DIM // 2
    return w.reshape(d_in, ATTN_HEADS, 2, 2, half).transpose(0, 1, 3, 2, 4).reshape(d_in, -1)


def kernel(x, mix_norm_g, mlp_norm_g, attn_w_qkv, attn_w_o, attn_lq1, attn_lk1, attn_lq2, attn_lk2,
           attn_subln_g, rec_w_x, rec_w_y, rec_conv_w, rec_conv_b, rec_w_a, rec_b_a, rec_w_i, rec_b_i,
           rec_lambda, rec_w_o, mlp_w1, mlp_w2, final_norm_g):
    batch, seq, d = x.shape
    x2d = x.reshape(batch * seq, d)
    row = lambda v: v.reshape(1, -1)

    lambda_init = 0.8 - 0.6 * math.exp(-0.3 * 0)
    w_qkv = attn_w_qkv[0]
    wq = _permute_heads(w_qkv[:, :d]).astype(BF16)
    wk = _permute_heads(w_qkv[:, d:2 * d]).astype(BF16)
    wvt = w_qkv[:, 2 * d:].T.astype(BF16)
    cos, sin = _rope_tables(seq)
    q, k, vt = _qkv_rope(x2d, row(mix_norm_g[0]), wq, wk, wvt, cos, sin, batch, seq)
    lqk = jnp.stack([attn_lq1[0], attn_lk1[0], attn_lq2[0], attn_lk2[0]])
    o = _diff_attn(lqk, attn_subln_g[0].reshape(V_DIM, 1),
                   q.reshape(batch, seq, d), k.reshape(batch, seq, d), vt, lambda_init)
    x2d = _proj_mlp(x2d, o.reshape(batch * seq, d), attn_w_o[0].astype(BF16), row(mlp_norm_g[0]),
                    mlp_w1[0].astype(BF16), mlp_w2[0].astype(BF16), row(final_norm_g), False)

    gated = _rglru(x2d.reshape(batch, seq, d), row(mix_norm_g[1]),
                   rec_w_x[0].astype(BF16), rec_w_y[0].astype(BF16), rec_conv_w[0], row(rec_conv_b[0]),
                   rec_w_a[0].astype(BF16), row(rec_b_a[0]), rec_w_i[0].astype(BF16), row(rec_b_i[0]),
                   row(rec_lambda[0]))
    out = _proj_mlp(x2d, gated.reshape(batch * seq, d), rec_w_o[0].astype(BF16), row(mlp_norm_g[1]),
                    mlp_w1[1].astype(BF16), mlp_w2[1].astype(BF16), row(final_norm_g), True)
    return out.reshape(batch, seq, d)
```

```python
import functools
import math

import jax
import jax.numpy as jnp
from jax import lax
from jax.experimental import pallas as pl
from jax.experimental.pallas import tpu as pltpu

F32 = jnp.float32
BF16 = jnp.bfloat16

D_MODEL = 1024
CHUNK = 64
ATTN_HEADS = 8
HEAD_DIM = 64
V_DIM = 128
VT_ROWS = V_DIM + 16
ROPE_THETA = 10000.0
RG_BLOCK = 256
RG_HEADS = 4
CONV_WIDTH = 4
RG_C = 8.0
D_FF = 4 * D_MODEL
NORM_EPS = 1e-6
SUBLN_EPS = 1e-5

LANES = 128
SUBLANES = 8
VMEM_LIMIT = 56 * 1024 * 1024

TM_QKV = 1024
TQ = 512
TK = TQ
N_STREAMS = 4
ROW_CHUNK = 64
BF16_ROWS = 2 * SUBLANES
MAX_JUMP = 16.0
assert LANES == 2 * CHUNK and TK == TQ
TM_MLP = 512
FF_CHUNK = 1024
TM_REC = 256
REC_STEPS = TM_REC // SUBLANES
REC_STREAMS = 4

NT = (((1,), (1,)), ((), ()))
Q_SCALE = (HEAD_DIM ** -0.5) * math.log2(math.e)


def _rms(x, g, eps):
    ms = jnp.mean(x * x, axis=-1, keepdims=True)
    return (x * lax.rsqrt(ms + eps)) * g


def _qkv_kernel(x_ref, g_ref, wq_ref, wk_ref, wvt_ref, cos_ref, sin_ref, q_ref, k_ref, vt_ref):
    y = _rms(x_ref[...], g_ref[...], NORM_EPS).astype(BF16)
    cos = cos_ref[...]
    sin = sin_ref[...]

    def rope(t):
        return t * cos + pltpu.roll(t, 64, 1) * sin

    q = jnp.dot(y, wq_ref[...], preferred_element_type=F32)
    for h in range(ATTN_HEADS):
        sl = slice(h * LANES, (h + 1) * LANES)
        q_ref[:, sl] = (rope(q[:, sl]) * Q_SCALE).astype(BF16)
    k = jnp.dot(y, wk_ref[...], preferred_element_type=F32)
    for h in range(ATTN_HEADS):
        sl = slice(h * LANES, (h + 1) * LANES)
        k_ref[:, sl] = rope(k[:, sl]).astype(BF16)
    vt = lax.dot_general(wvt_ref[...], y, NT, preferred_element_type=F32)
    for h in range(ATTN_HEADS):
        for t in range(TM_QKV // TK):
            vt_ref[0, h, t, :V_DIM, :] = vt[h * V_DIM:(h + 1) * V_DIM, t * TK:(t + 1) * TK].astype(BF16)
            vt_ref[0, h, t, V_DIM:, :] = jnp.ones((VT_ROWS - V_DIM, TK), BF16)


def _qkv_rope(x2d, g, wq, wk, wvt, cos, sin, batch, seq):
    n_tok = x2d.shape[0]
    tiles_per_seq = seq // TM_QKV
    const = lambda i: (0, 0)
    return pl.pallas_call(
        _qkv_kernel,
        grid=(n_tok // TM_QKV,),
        in_specs=[
            pl.BlockSpec((TM_QKV, D_MODEL), lambda i: (i, 0)),
            pl.BlockSpec((1, D_MODEL), const),
            pl.BlockSpec((D_MODEL, D_MODEL), const),
            pl.BlockSpec((D_MODEL, D_MODEL), const),
            pl.BlockSpec((D_MODEL, D_MODEL), const),
            pl.BlockSpec((TM_QKV, LANES), lambda i: (i % tiles_per_seq, 0)),
            pl.BlockSpec((TM_QKV, LANES), lambda i: (i % tiles_per_seq, 0)),
        ],
        out_specs=[
            pl.BlockSpec((TM_QKV, D_MODEL), lambda i: (i, 0)),
            pl.BlockSpec((TM_QKV, D_MODEL), lambda i: (i, 0)),
            pl.BlockSpec((1, ATTN_HEADS, TM_QKV // TK, VT_ROWS, TK),
                         lambda i: (i // tiles_per_seq, 0, i % tiles_per_seq, 0, 0)),
        ],
        out_shape=[
            jax.ShapeDtypeStruct((n_tok, D_MODEL), BF16),
            jax.ShapeDtypeStruct((n_tok, D_MODEL), BF16),
            jax.ShapeDtypeStruct((batch, ATTN_HEADS, seq // TK, VT_ROWS, TK), BF16),
        ],
        compiler_params=pltpu.CompilerParams(
            dimension_semantics=("arbitrary",), vmem_limit_bytes=VMEM_LIMIT),
        name="qkv_rope",
    )(x2d, g, wq, wk, wvt, cos, sin)


def _attn_kernel(lqk_ref, g_ref, q_ref, k_ref, vt_ref, o_ref, m_ref, l_ref, acc_ref, alpha_ref, p_ref,
                 jump_ref, *, lambda_init):
    qi = pl.program_id(2)

    lane = lax.broadcasted_iota(jnp.int32, (BF16_ROWS, LANES), 1)
    is_map1 = (lane // (HEAD_DIM // 2)) % 2 == 0
    keep1 = jnp.where(is_map1, 1.0, 0.0).astype(BF16)
    keep2 = jnp.where(is_map1, 0.0, 1.0).astype(BF16)
    keep1 = jnp.concatenate([keep1] * (TQ // BF16_ROWS), axis=0)
    keep2 = jnp.concatenate([keep2] * (TQ // BF16_ROWS), axis=0)
    q_st = []
    for s in range(N_STREAMS):
        q = q_ref[0, :, s * LANES:(s + 1) * LANES]
        q_st.append(jnp.concatenate([q * keep1, q * keep2], axis=0))

    def score_matmul(s, j):
        rows = pl.ds(pl.multiple_of(j * TK, TK), TK)
        return lax.dot_general(k_ref[0, rows, s * LANES:(s + 1) * LANES], q_st[s], NT,
                               preferred_element_type=F32)

    def value_matmul(s, j):
        return jnp.dot(vt_ref[0, s, j], p_ref[s], preferred_element_type=F32)

    def accumulate(s, pv, stale):
        alpha = alpha_ref[s]
        if stale:
            acc_ref[s] = alpha * (acc_ref[s] + pv[:V_DIM, :])
            l_ref[s] = alpha * (l_ref[s] + pv[V_DIM:V_DIM + 1, :])
        else:
            acc_ref[s] = alpha * acc_ref[s] + pv[:V_DIM, :]
            l_ref[s] = alpha * l_ref[s] + pv[V_DIM:V_DIM + 1, :]

    def matmuls(j, pending):
        scores, pv = [], {}
        for s in range(N_STREAMS):
            scores.append(score_matmul(s, j))
            if pending is not None and s >= 1:
                pv[s - 1] = value_matmul(s - 1, pending)
        if pending is not None:
            pv[N_STREAMS - 1] = value_matmul(N_STREAMS - 1, pending)
        return scores, pv

    upper_half = lax.broadcasted_iota(jnp.int32, (BF16_ROWS, LANES), 1) >= CHUNK

    def two_pass_tile(j, own, pending, stale):
        scores, pv = matmuls(j, pending)
        width = LANES if own else 2 * TQ
        lane_blocks = [slice(c, c + width) for c in range(0, 2 * TQ, width)]
        cache = {}

        def block(s, g, cols):
            if (s, g, cols.start) not in cache:
                blk = scores[s][g:g + BF16_ROWS, cols]
                if own:
                    key_chunk = g // CHUNK
                    qry_chunk = (cols.start % TQ) // CHUNK
                    if qry_chunk + 1 < key_chunk:
                        blk = None
                    elif qry_chunk < key_chunk:
                        blk = jnp.where(upper_half, blk, -jnp.inf)
                cache[s, g, cols.start] = blk
            return cache[s, g, cols.start]

        def row_groups(c):
            return range(c * ROW_CHUNK, (c + 1) * ROW_CHUNK, BF16_ROWS)

        m_new = {}

        def max_pass(s):
            if pending is not None:
                accumulate(s, pv[s], stale)
            mx = {}
            for c in range(TK // ROW_CHUNK):
                for g in row_groups(c):
                    for cols in lane_blocks:
                        blk = block(s, g, cols)
                        if blk is not None:
                            seen = mx.get(cols.start)
                            mx[cols.start] = blk if seen is None else jnp.maximum(seen, blk)
                yield
            m_old = m_ref[s]
            col_max = jnp.concatenate(
                [jnp.max(mx[cols.start], axis=0, keepdims=True) for cols in lane_blocks], axis=1)
            m_new[s] = jnp.maximum(m_old, col_max)
            alpha_ref[s] = jnp.ones_like(m_old) if stale else jnp.exp2(m_old - m_new[s])
            m_ref[s] = m_new[s]

        def exp_pass(s):
            for c in range(TK // ROW_CHUNK):
                for g in row_groups(c):
                    for cols in lane_blocks:
                        blk = block(s, g, cols)
                        if blk is None:
                            p = jnp.zeros((BF16_ROWS, width), BF16)
                        else:
                            p = jnp.exp2(blk - m_new[s][:, cols]).astype(BF16)
                        p_ref[s, g:g + BF16_ROWS, cols] = p
                yield

        for _ in max_pass(0):
            pass
        for s in range(N_STREAMS):
            ahead = max_pass(s + 1) if s + 1 < N_STREAMS else iter(())
            for _ in exp_pass(s):
                next(ahead, None)
            for _ in ahead:
                pass

    def one_pass_tile(j, pending):
        scores, pv = matmuls(j, pending)
        for s in range(N_STREAMS):
            accumulate(s, pv[s], True)
            m_old = m_ref[s]
            mx = None
            for g in range(0, TK, BF16_ROWS):
                blk = scores[s][g:g + BF16_ROWS, :]
                mx = blk if mx is None else jnp.maximum(mx, blk)
                p_ref[s, g:g + BF16_ROWS, :] = jnp.exp2(blk - m_old).astype(BF16)
            col_max = jnp.max(mx, axis=0, keepdims=True)
            jump_ref[s] = jnp.maximum(jump_ref[s], col_max - m_old)
            m_new = jnp.maximum(m_old, col_max)
            alpha_ref[s] = jnp.exp2(m_old - m_new)
            m_ref[s] = m_new

    def attend(one_pass):
        m_ref[...] = jnp.full(m_ref.shape, -1e30, F32)
        l_ref[...] = jnp.zeros(l_ref.shape, F32)
        acc_ref[...] = jnp.zeros(acc_ref.shape, F32)
        two_pass_tile(qi, True, None, one_pass)

        def body(j, pending):
            if one_pass:
                one_pass_tile(j, pending)
            else:
                two_pass_tile(j, False, pending, False)
            return j

        pending = lax.fori_loop(0, qi, body, qi)
        for s in range(N_STREAMS):
            accumulate(s, value_matmul(s, pending), one_pass)

    jump_ref[...] = jnp.zeros(jump_ref.shape, F32)
    attend(True)

    @pl.when(jnp.max(jump_ref[...]) > MAX_JUMP)
    def _():
        attend(False)

    lqk = lqk_ref[...]
    lam = (jnp.exp(jnp.sum(lqk[0:1] * lqk[1:2], axis=1, keepdims=True))
           - jnp.exp(jnp.sum(lqk[2:3] * lqk[3:4], axis=1, keepdims=True))
           + lambda_init)
    for s in range(N_STREAMS):
        acc = acc_ref[s]
        l = l_ref[s]
        o = acc[:, :TQ] / l[:, :TQ] - lam * (acc[:, TQ:] / l[:, TQ:])
        ms = jnp.mean(o * o, axis=0, keepdims=True)
        y = ((o * lax.rsqrt(ms + SUBLN_EPS)) * g_ref[...]) * (1.0 - lambda_init)
        o_ref[0, :, s * LANES:(s + 1) * LANES] = y.T.astype(BF16)


def _diff_attn(lqk, g_col, q, k, vt, lambda_init):
    batch, seq, _ = q.shape
    width = N_STREAMS * LANES
    kernel = functools.partial(_attn_kernel, lambda_init=lambda_init)
    return pl.pallas_call(
        kernel,
        grid=(batch, ATTN_HEADS // N_STREAMS, seq // TQ),
        in_specs=[
            pl.BlockSpec((4, HEAD_DIM), lambda b, h, i: (0, 0)),
            pl.BlockSpec((V_DIM, 1), lambda b, h, i: (0, 0)),
            pl.BlockSpec((1, TQ, width), lambda b, h, i: (b, i, h)),
            pl.BlockSpec((1, seq, width), lambda b, h, i: (b, 0, h)),
            pl.BlockSpec((1, N_STREAMS, seq // TK, VT_ROWS, TK), lambda b, h, i: (b, h, 0, 0, 0)),
        ],
        out_specs=pl.BlockSpec((1, TQ, width), lambda b, h, i: (b, i, h)),
        out_shape=jax.ShapeDtypeStruct((batch, seq, ATTN_HEADS * V_DIM), BF16),
        scratch_shapes=[
            pltpu.VMEM((N_STREAMS, 1, 2 * TQ), F32),
            pltpu.VMEM((N_STREAMS, 1, 2 * TQ), F32),
            pltpu.VMEM((N_STREAMS, V_DIM, 2 * TQ), F32),
            pltpu.VMEM((N_STREAMS, 1, 2 * TQ), F32),
            pltpu.VMEM((N_STREAMS, TK, 2 * TQ), BF16),
            pltpu.VMEM((N_STREAMS, 1, 2 * TQ), F32),
        ],
        compiler_params=pltpu.CompilerParams(
            dimension_semantics=("arbitrary", "arbitrary", "arbitrary"),
            vmem_limit_bytes=VMEM_LIMIT),
        name="diff_attn",
    )(lqk, g_col, q, k, vt)


def _proj_mlp_kernel(x_ref, a_ref, wo_ref, g_ref, w1_ref, w2_ref, gf_ref, out_ref, *, final_norm):
    x1 = x_ref[...] + jnp.dot(a_ref[...], wo_ref[...], preferred_element_type=F32)
    h = _rms(x1, g_ref[...], NORM_EPS).astype(BF16)
    acc = x1
    for c in range(D_FF // FF_CHUNK):
        sl = slice(c * FF_CHUNK, (c + 1) * FF_CHUNK)
        u = jnp.dot(h, w1_ref[:, sl], preferred_element_type=F32)
        u = jnp.square(jnp.maximum(u, 0.0)).astype(BF16)
        acc = acc + jnp.dot(u, w2_ref[sl, :], preferred_element_type=F32)
    if final_norm:
        acc = _rms(acc, gf_ref[...], NORM_EPS)
    out_ref[...] = acc


def _proj_mlp(x2d, a2d, wo, g, w1, w2, gf, final_norm):
    n_tok = x2d.shape[0]
    const = lambda i: (0, 0)
    resident = pl.Buffered(1)
    kernel = functools.partial(_proj_mlp_kernel, final_norm=final_norm)
    return pl.pallas_call(
        kernel,
        grid=(n_tok // TM_MLP,),
        in_specs=[
            pl.BlockSpec((TM_MLP, D_MODEL), lambda i: (i, 0)),
            pl.BlockSpec((TM_MLP, D_MODEL), lambda i: (i, 0)),
            pl.BlockSpec((D_MODEL, D_MODEL), const, pipeline_mode=resident),
            pl.BlockSpec((1, D_MODEL), const),
            pl.BlockSpec((D_MODEL, D_FF), const, pipeline_mode=resident),
            pl.BlockSpec((D_FF, D_MODEL), const, pipeline_mode=resident),
            pl.BlockSpec((1, D_MODEL), const),
        ],
        out_specs=pl.BlockSpec((TM_MLP, D_MODEL), lambda i: (i, 0)),
        out_shape=jax.ShapeDtypeStruct((n_tok, D_MODEL), F32),
        compiler_params=pltpu.CompilerParams(
            dimension_semantics=("arbitrary",), vmem_limit_bytes=VMEM_LIMIT),
        name="proj_mlp_final" if final_norm else "proj_mlp",
    )(x2d, a2d, wo, g, w1, w2, gf)


def _rglru_kernel(x_ref, g_ref, perm_ref, unperm_ref, wx_ref, wy_ref, cw_ref, cb_ref, wa_ref, ba_ref,
                  wi_ref, bi_ref, lam_ref, out_ref, tail_ref, h_ref):
    t = pl.program_id(1)
    n_tail = (CONV_WIDTH - 1) * SUBLANES

    @pl.when(t == 0)
    def _():
        tail_ref[...] = jnp.zeros(tail_ref.shape, F32)
        h_ref[...] = jnp.zeros(h_ref.shape, F32)

    group = lambda v, r: v[r * SUBLANES:(r + 1) * SUBLANES, :]
    sub = lax.broadcasted_iota(jnp.int32, (SUBLANES, D_MODEL), 0)

    def project(s):
        h = _rms(x_ref[s], g_ref[...], NORM_EPS).astype(BF16)
        h = jnp.dot(perm_ref[...], h, preferred_element_type=F32).astype(BF16)
        gate = jax.nn.gelu(jnp.dot(h, wy_ref[...], preferred_element_type=F32))
        xp = jnp.dot(h, wx_ref[...], preferred_element_type=F32)
        return gate, xp

    def conv_and_gates(s, xp):
        lead = []
        for k in range(CONV_WIDTH - 1):
            cur = group(xp, REC_STEPS - (CONV_WIDTH - 1) + k)
            prev_last = tail_ref[s, (k + 1) * SUBLANES - 1:(k + 1) * SUBLANES, :]
            lead.append(jnp.where(sub == 0, prev_last, pltpu.roll(cur, 1, 0)))
        tail_ref[s] = xp[TM_REC - n_tail:, :]
        ext = jnp.concatenate(lead + [xp], axis=0)
        cw = cw_ref[...]
        xb = cb_ref[...]
        for j in range(CONV_WIDTH):
            xb = xb + ext[j * SUBLANES:j * SUBLANES + TM_REC, :] * cw[j:j + 1, :]
        xb16 = xb.astype(BF16)

        def block_diag(w_ref):
            return jnp.concatenate(
                [jnp.dot(xb16[:, n * RG_BLOCK:(n + 1) * RG_BLOCK], w_ref[n],
                         preferred_element_type=F32) for n in range(RG_HEADS)], axis=1)

        return xb, block_diag(wa_ref), block_diag(wi_ref)

    def recur(s, gate, xb, ra, ri):
        r = jax.nn.sigmoid(ra + ba_ref[...])
        i = jax.nn.sigmoid(ri + bi_ref[...])
        neg_lam = -lam_ref[...]
        softplus = jnp.maximum(neg_lam, 0.0) + jnp.log1p(jnp.exp(-jnp.abs(neg_lam)))
        log_a = (-RG_C * r) * softplus
        a = jnp.exp(log_a)
        u = jnp.sqrt(1.0 - a * a) * (i * xb)
        hz, pz = [group(u, 0)], [group(a, 0)]
        for k in range(1, REC_STEPS):
            a_k = group(a, k)
            hz.append(a_k * hz[-1] + group(u, k))
            pz.append(a_k * pz[-1])
        start = h_ref[s]
        starts = []
        for b in range(SUBLANES):
            starts.append(start)
            start = hz[-1][b:b + 1, :] + pz[-1][b:b + 1, :] * start
        h_ref[s] = start
        starts = jnp.concatenate(starts, axis=0)
        y = jnp.concatenate([(hz[k] + pz[k] * starts) * group(gate, k) for k in range(REC_STEPS)],
                            axis=0).astype(BF16)
        out_ref[s] = jnp.dot(unperm_ref[...], y, preferred_element_type=F32).astype(BF16)

    streams = range(REC_STREAMS)
    proj = [project(s) for s in streams]
    mid = [conv_and_gates(s, proj[s][1]) for s in streams]
    for s in streams:
        recur(s, proj[s][0], *mid[s])


def _rglru(x, g, wx, wy, cw, cb, wa, ba, wi, bi, lam):
    batch, seq, _ = x.shape
    src = (jnp.arange(TM_REC) % SUBLANES) * REC_STEPS + jnp.arange(TM_REC) // SUBLANES
    perm = (src[:, None] == jnp.arange(TM_REC)[None, :]).astype(BF16)
    c2 = lambda b, t: (0, 0)
    c3 = lambda b, t: (0, 0, 0)
    return pl.pallas_call(
        _rglru_kernel,
        grid=(batch // REC_STREAMS, seq // TM_REC),
        in_specs=[
            pl.BlockSpec((REC_STREAMS, TM_REC, D_MODEL), lambda b, t: (b, t, 0)),
            pl.BlockSpec((1, D_MODEL), c2),
            pl.BlockSpec((TM_REC, TM_REC), c2),
            pl.BlockSpec((TM_REC, TM_REC), c2),
            pl.BlockSpec((D_MODEL, D_MODEL), c2),
            pl.BlockSpec((D_MODEL, D_MODEL), c2),
            pl.BlockSpec((CONV_WIDTH, D_MODEL), c2),
            pl.BlockSpec((1, D_MODEL), c2),
            pl.BlockSpec((RG_HEADS, RG_BLOCK, RG_BLOCK), c3),
            pl.BlockSpec((1, D_MODEL), c2),
            pl.BlockSpec((RG_HEADS, RG_BLOCK, RG_BLOCK), c3),
            pl.BlockSpec((1, D_MODEL), c2),
            pl.BlockSpec((1, D_MODEL), c2),
        ],
        out_specs=pl.BlockSpec((REC_STREAMS, TM_REC, D_MODEL), lambda b, t: (b, t, 0)),
        out_shape=jax.ShapeDtypeStruct((batch, seq, D_MODEL), BF16),
        scratch_shapes=[
            pltpu.VMEM((REC_STREAMS, (CONV_WIDTH - 1) * SUBLANES, D_MODEL), F32),
            pltpu.VMEM((REC_STREAMS, 1, D_MODEL), F32),
        ],
        compiler_params=pltpu.CompilerParams(
            dimension_semantics=("arbitrary", "arbitrary"), vmem_limit_bytes=VMEM_LIMIT),
        name="rglru",
    )(x, g, perm, perm.T, wx, wy, cw, cb, wa, ba, wi, bi, lam)


def _rope_tables(seq):
    half = HEAD_DIM // 2
    inv_freq = 1.0 / (ROPE_THETA ** (jnp.arange(0, HEAD_DIM, 2, dtype=F32) / HEAD_DIM))
    ang = jnp.arange(seq, dtype=jnp.int32).astype(F32)[:, None] * inv_freq[None, :]
    cos = jnp.tile(jnp.cos(ang), (1, LANES // half))
    sin = jnp.tile(jnp.sin(ang), (1, LANES // half))
    sign = jnp.where(jnp.arange(LANES) < LANES // 2, -1.0, 1.0).astype(F32)
    return cos, sin * sign[None, :]


def _permute_heads(w):
    d_in = w.shape[0]
    half = HEAD_DIM // 2
    return w.reshape(d_in, ATTN_HEADS, 2, 2, half).transpose(0, 1, 3, 2, 4).reshape(d_in, -1)


def kernel(x, mix_norm_g, mlp_norm_g, attn_w_qkv, attn_w_o, attn_lq1, attn_lk1, attn_lq2, attn_lk2,
           attn_subln_g, rec_w_x, rec_w_y, rec_conv_w, rec_conv_b, rec_w_a, rec_b_a, rec_w_i, rec_b_i,
           rec_lambda, rec_w_o, mlp_w1, mlp_w2, final_norm_g):
    batch, seq, d = x.shape
    x2d = x.reshape(batch * seq, d)
    row = lambda v: v.reshape(1, -1)

    lambda_init = 0.8 - 0.6 * math.exp(-0.3 * 0)
    w_qkv = attn_w_qkv[0]
    wq = _permute_heads(w_qkv[:, :d]).astype(BF16)
    wk = _permute_heads(w_qkv[:, d:2 * d]).astype(BF16)
    wvt = w_qkv[:, 2 * d:].T.astype(BF16)
    cos, sin = _rope_tables(seq)
    q, k, vt = _qkv_rope(x2d, row(mix_norm_g[0]), wq, wk, wvt, cos, sin, batch, seq)
    lqk = jnp.stack([attn_lq1[0], attn_lk1[0], attn_lq2[0], attn_lk2[0]])
    o = _diff_attn(lqk, attn_subln_g[0].reshape(V_DIM, 1),
                   q.reshape(batch, seq, d), k.reshape(batch, seq, d), vt, lambda_init)
    x2d = _proj_mlp(x2d, o.reshape(batch * seq, d), attn_w_o[0].astype(BF16), row(mlp_norm_g[0]),
                    mlp_w1[0].astype(BF16), mlp_w2[0].astype(BF16), row(final_norm_g), False)

    gated = _rglru(x2d.reshape(batch, seq, d), row(mix_norm_g[1]),
                   rec_w_x[0].astype(BF16), rec_w_y[0].astype(BF16), rec_conv_w[0], row(rec_conv_b[0]),
                   rec_w_a[0].astype(BF16), row(rec_b_a[0]), rec_w_i[0].astype(BF16), row(rec_b_i[0]),
                   row(rec_lambda[0]))
    out = _proj_mlp(x2d, gated.reshape(batch * seq, d), rec_w_o[0].astype(BF16), row(mlp_norm_g[1]),
                    mlp_w1[1].astype(BF16), mlp_w2[1].astype(BF16), row(final_norm_g), True)
    return out.reshape(batch, seq, d)
```

```python
import functools
import math

import jax
import jax.numpy as jnp
from jax import lax
from jax.experimental import pallas as pl
from jax.experimental.pallas import tpu as pltpu

F32 = jnp.float32
BF16 = jnp.bfloat16

D_MODEL = 1024
CHUNK = 64
ATTN_HEADS = 8
HEAD_DIM = 64
V_DIM = 128
VT_ROWS = V_DIM + 16
ROPE_THETA = 10000.0
RG_BLOCK = 256
RG_HEADS = 4
CONV_WIDTH = 4
RG_C = 8.0
D_FF = 4 * D_MODEL
NORM_EPS = 1e-6
SUBLN_EPS = 1e-5

LANES = 128
SUBLANES = 8
VMEM_LIMIT = 56 * 1024 * 1024

TM_QKV = 1024
TQ = 512
TK = TQ
N_STREAMS = 4
ROW_CHUNK = 64
BF16_ROWS = 2 * SUBLANES
assert LANES == 2 * CHUNK and TK == TQ
TM_MLP = 512
FF_CHUNK = 1024
TM_REC = 256
REC_STEPS = TM_REC // SUBLANES
REC_STREAMS = 4

NT = (((1,), (1,)), ((), ()))
Q_SCALE = (HEAD_DIM ** -0.5) * math.log2(math.e)


def _rms(x, g, eps):
    ms = jnp.mean(x * x, axis=-1, keepdims=True)
    return (x * lax.rsqrt(ms + eps)) * g


def _gelu_tanh(x):
    c0 = math.sqrt(2.0 / math.pi)
    inner = x * (c0 + (c0 * 0.044715) * (x * x))
    return (0.5 * x) * (1.0 + jnp.tanh(inner))


def _qkv_kernel(x_ref, g_ref, wq_ref, wk_ref, wvt_ref, cos_ref, sin_ref, q_ref, k_ref, vt_ref):
    y = _rms(x_ref[...], g_ref[...], NORM_EPS).astype(BF16)
    cos = cos_ref[...]
    sin = sin_ref[...]

    def rope(t):
        return t * cos + pltpu.roll(t, 64, 1) * sin

    q = jnp.dot(y, wq_ref[...], preferred_element_type=F32)
    for h in range(ATTN_HEADS):
        sl = slice(h * LANES, (h + 1) * LANES)
        q_ref[:, sl] = (rope(q[:, sl]) * Q_SCALE).astype(BF16)
    k = jnp.dot(y, wk_ref[...], preferred_element_type=F32)
    for h in range(ATTN_HEADS):
        sl = slice(h * LANES, (h + 1) * LANES)
        k_ref[:, sl] = rope(k[:, sl]).astype(BF16)
    vt = lax.dot_general(wvt_ref[...], y, NT, preferred_element_type=F32)
    for h in range(ATTN_HEADS):
        for t in range(TM_QKV // TK):
            vt_ref[0, h, t, :V_DIM, :] = vt[h * V_DIM:(h + 1) * V_DIM, t * TK:(t + 1) * TK].astype(BF16)
            vt_ref[0, h, t, V_DIM:, :] = jnp.ones((VT_ROWS - V_DIM, TK), BF16)


def _qkv_rope(x2d, g, wq, wk, wvt, cos, sin, batch, seq):
    n_tok = x2d.shape[0]
    tiles_per_seq = seq // TM_QKV
    const = lambda i: (0, 0)
    return pl.pallas_call(
        _qkv_kernel,
        grid=(n_tok // TM_QKV,),
        in_specs=[
            pl.BlockSpec((TM_QKV, D_MODEL), lambda i: (i, 0)),
            pl.BlockSpec((1, D_MODEL), const),
            pl.BlockSpec((D_MODEL, D_MODEL), const),
            pl.BlockSpec((D_MODEL, D_MODEL), const),
            pl.BlockSpec((D_MODEL, D_MODEL), const),
            pl.BlockSpec((TM_QKV, LANES), lambda i: (i % tiles_per_seq, 0)),
            pl.BlockSpec((TM_QKV, LANES), lambda i: (i % tiles_per_seq, 0)),
        ],
        out_specs=[
            pl.BlockSpec((TM_QKV, D_MODEL), lambda i: (i, 0)),
            pl.BlockSpec((TM_QKV, D_MODEL), lambda i: (i, 0)),
            pl.BlockSpec((1, ATTN_HEADS, TM_QKV // TK, VT_ROWS, TK),
                         lambda i: (i // tiles_per_seq, 0, i % tiles_per_seq, 0, 0)),
        ],
        out_shape=[
            jax.ShapeDtypeStruct((n_tok, D_MODEL), BF16),
            jax.ShapeDtypeStruct((n_tok, D_MODEL), BF16),
            jax.ShapeDtypeStruct((batch, ATTN_HEADS, seq // TK, VT_ROWS, TK), BF16),
        ],
        compiler_params=pltpu.CompilerParams(
            dimension_semantics=("arbitrary",), vmem_limit_bytes=VMEM_LIMIT),
        name="qkv_rope",
    )(x2d, g, wq, wk, wvt, cos, sin)


def _attn_kernel(lqk_ref, g_ref, q_ref, k_ref, vt_ref, o_ref, m_ref, l_ref, acc_ref, alpha_ref, p_ref,
                 *, lambda_init):
    qi = pl.program_id(2)

    lane = lax.broadcasted_iota(jnp.int32, (BF16_ROWS, LANES), 1)
    is_map1 = (lane // (HEAD_DIM // 2)) % 2 == 0
    keep1 = jnp.where(is_map1, 1.0, 0.0).astype(BF16)
    keep2 = jnp.where(is_map1, 0.0, 1.0).astype(BF16)
    keep1 = jnp.concatenate([keep1] * (TQ // BF16_ROWS), axis=0)
    keep2 = jnp.concatenate([keep2] * (TQ // BF16_ROWS), axis=0)
    q_st = []
    for s in range(N_STREAMS):
        q = q_ref[0, :, s * LANES:(s + 1) * LANES]
        q_st.append(jnp.concatenate([q * keep1, q * keep2], axis=0))

    def score_matmul(s, j):
        rows = pl.ds(pl.multiple_of(j * TK, TK), TK)
        return lax.dot_general(k_ref[0, rows, s * LANES:(s + 1) * LANES], q_st[s], NT,
                               preferred_element_type=F32)

    def value_matmul(s, j):
        return jnp.dot(vt_ref[0, s, j], p_ref[s], preferred_element_type=F32)

    def accumulate(s, pv):
        acc_ref[s] = alpha_ref[s] * acc_ref[s] + pv[:V_DIM, :]
        l_ref[s] = alpha_ref[s] * l_ref[s] + pv[V_DIM:V_DIM + 1, :]

    def matmuls(j, pending):
        scores, pv = [], {}
        for s in range(N_STREAMS):
            scores.append(score_matmul(s, j))
            if pending is not None and s >= 1:
                pv[s - 1] = value_matmul(s - 1, pending)
        if pending is not None:
            pv[N_STREAMS - 1] = value_matmul(N_STREAMS - 1, pending)
        return scores, pv

    upper_half = lax.broadcasted_iota(jnp.int32, (BF16_ROWS, LANES), 1) >= CHUNK

    def key_tile(j, own, pending):
        scores, pv = matmuls(j, pending)
        width = LANES if own else 2 * TQ
        lane_blocks = [slice(c, c + width) for c in range(0, 2 * TQ, width)]
        cache = {}

        def block(s, g, cols):
            if (s, g, cols.start) not in cache:
                blk = scores[s][g:g + BF16_ROWS, cols]
                if own:
                    key_chunk = g // CHUNK
                    qry_chunk = (cols.start % TQ) // CHUNK
                    if qry_chunk + 1 < key_chunk:
                        blk = None
                    elif qry_chunk < key_chunk:
                        blk = jnp.where(upper_half, blk, -jnp.inf)
                cache[s, g, cols.start] = blk
            return cache[s, g, cols.start]

        def row_groups(c):
            return range(c * ROW_CHUNK, (c + 1) * ROW_CHUNK, BF16_ROWS)

        m_new = {}

        def max_pass(s):
            if pending is not None:
                accumulate(s, pv[s])
            mx = {}
            for c in range(TK // ROW_CHUNK):
                for g in row_groups(c):
                    for cols in lane_blocks:
                        blk = block(s, g, cols)
                        if blk is not None:
                            seen = mx.get(cols.start)
                            mx[cols.start] = blk if seen is None else jnp.maximum(seen, blk)
                yield
            m_old = m_ref[s]
            col_max = jnp.concatenate(
                [jnp.max(mx[cols.start], axis=0, keepdims=True) for cols in lane_blocks], axis=1)
            m_new[s] = jnp.maximum(m_old, col_max)
            alpha_ref[s] = jnp.exp2(m_old - m_new[s])
            m_ref[s] = m_new[s]

        def exp_pass(s):
            for c in range(TK // ROW_CHUNK):
                for g in row_groups(c):
                    for cols in lane_blocks:
                        blk = block(s, g, cols)
                        if blk is None:
                            p = jnp.zeros((BF16_ROWS, width), BF16)
                        else:
                            p = jnp.exp2(blk - m_new[s][:, cols]).astype(BF16)
                        p_ref[s, g:g + BF16_ROWS, cols] = p
                yield

        for _ in max_pass(0):
            pass
        for s in range(N_STREAMS):
            ahead = max_pass(s + 1) if s + 1 < N_STREAMS else iter(())
            for _ in exp_pass(s):
                next(ahead, None)
            for _ in ahead:
                pass

    m_ref[...] = jnp.full(m_ref.shape, -1e30, F32)
    l_ref[...] = jnp.zeros(l_ref.shape, F32)
    acc_ref[...] = jnp.zeros(acc_ref.shape, F32)
    key_tile(qi, True, None)

    def body(j, pending):
        key_tile(j, False, pending)
        return j

    pending = lax.fori_loop(0, qi, body, qi)
    for s in range(N_STREAMS):
        accumulate(s, value_matmul(s, pending))

    lqk = lqk_ref[...]
    lam = (jnp.exp(jnp.sum(lqk[0:1] * lqk[1:2], axis=1, keepdims=True))
           - jnp.exp(jnp.sum(lqk[2:3] * lqk[3:4], axis=1, keepdims=True))
           + lambda_init)
    for s in range(N_STREAMS):
        acc = acc_ref[s]
        l = l_ref[s]
        o = acc[:, :TQ] / l[:, :TQ] - lam * (acc[:, TQ:] / l[:, TQ:])
        ms = jnp.mean(o * o, axis=0, keepdims=True)
        y = ((o * lax.rsqrt(ms + SUBLN_EPS)) * g_ref[...]) * (1.0 - lambda_init)
        o_ref[0, :, s * LANES:(s + 1) * LANES] = y.T.astype(BF16)


def _diff_attn(lqk, g_col, q, k, vt, lambda_init):
    batch, seq, _ = q.shape
    width = N_STREAMS * LANES
    kernel = functools.partial(_attn_kernel, lambda_init=lambda_init)
    return pl.pallas_call(
        kernel,
        grid=(batch, ATTN_HEADS // N_STREAMS, seq // TQ),
        in_specs=[
            pl.BlockSpec((4, HEAD_DIM), lambda b, h, i: (0, 0)),
            pl.BlockSpec((V_DIM, 1), lambda b, h, i: (0, 0)),
            pl.BlockSpec((1, TQ, width), lambda b, h, i: (b, i, h)),
            pl.BlockSpec((1, seq, width), lambda b, h, i: (b, 0, h)),
            pl.BlockSpec((1, N_STREAMS, seq // TK, VT_ROWS, TK), lambda b, h, i: (b, h, 0, 0, 0)),
        ],
        out_specs=pl.BlockSpec((1, TQ, width), lambda b, h, i: (b, i, h)),
        out_shape=jax.ShapeDtypeStruct((batch, seq, ATTN_HEADS * V_DIM), BF16),
        scratch_shapes=[
            pltpu.VMEM((N_STREAMS, 1, 2 * TQ), F32),
            pltpu.VMEM((N_STREAMS, 1, 2 * TQ), F32),
            pltpu.VMEM((N_STREAMS, V_DIM, 2 * TQ), F32),
            pltpu.VMEM((N_STREAMS, 1, 2 * TQ), F32),
            pltpu.VMEM((N_STREAMS, TK, 2 * TQ), BF16),
        ],
        compiler_params=pltpu.CompilerParams(
            dimension_semantics=("arbitrary", "arbitrary", "arbitrary"),
            vmem_limit_bytes=VMEM_LIMIT),
        name="diff_attn",
    )(lqk, g_col, q, k, vt)


def _proj_mlp_kernel(x_ref, a_ref, wo_ref, g_ref, w1_ref, w2_ref, gf_ref, out_ref, *, final_norm):
    x1 = x_ref[...] + jnp.dot(a_ref[...], wo_ref[...], preferred_element_type=F32)
    h = _rms(x1, g_ref[...], NORM_EPS).astype(BF16)
    acc = x1
    for c in range(D_FF // FF_CHUNK):
        sl = slice(c * FF_CHUNK, (c + 1) * FF_CHUNK)
        u = jnp.dot(h, w1_ref[:, sl], preferred_element_type=F32)
        u = jnp.square(jnp.maximum(u, 0.0)).astype(BF16)
        acc = acc + jnp.dot(u, w2_ref[sl, :], preferred_element_type=F32)
    if final_norm:
        acc = _rms(acc, gf_ref[...], NORM_EPS)
    out_ref[...] = acc


def _proj_mlp(x2d, a2d, wo, g, w1, w2, gf, final_norm):
    n_tok = x2d.shape[0]
    const = lambda i: (0, 0)
    resident = pl.Buffered(1)
    kernel = functools.partial(_proj_mlp_kernel, final_norm=final_norm)
    return pl.pallas_call(
        kernel,
        grid=(n_tok // TM_MLP,),
        in_specs=[
            pl.BlockSpec((TM_MLP, D_MODEL), lambda i: (i, 0)),
            pl.BlockSpec((TM_MLP, D_MODEL), lambda i: (i, 0)),
            pl.BlockSpec((D_MODEL, D_MODEL), const, pipeline_mode=resident),
            pl.BlockSpec((1, D_MODEL), const),
            pl.BlockSpec((D_MODEL, D_FF), const, pipeline_mode=resident),
            pl.BlockSpec((D_FF, D_MODEL), const, pipeline_mode=resident),
            pl.BlockSpec((1, D_MODEL), const),
        ],
        out_specs=pl.BlockSpec((TM_MLP, D_MODEL), lambda i: (i, 0)),
        out_shape=jax.ShapeDtypeStruct((n_tok, D_MODEL), F32),
        compiler_params=pltpu.CompilerParams(
            dimension_semantics=("arbitrary",), vmem_limit_bytes=VMEM_LIMIT),
        name="proj_mlp_final" if final_norm else "proj_mlp",
    )(x2d, a2d, wo, g, w1, w2, gf)


def _rglru_kernel(x_ref, g_ref, perm_ref, unperm_ref, wx_ref, wy_ref, cw_ref, cb_ref, wa_ref, ba_ref,
                  wi_ref, bi_ref, lam_ref, out_ref, tail_ref, h_ref):
    t = pl.program_id(1)
    n_tail = (CONV_WIDTH - 1) * SUBLANES

    @pl.when(t == 0)
    def _():
        tail_ref[...] = jnp.zeros(tail_ref.shape, F32)
        h_ref[...] = jnp.zeros(h_ref.shape, F32)

    group = lambda v, r: v[r * SUBLANES:(r + 1) * SUBLANES, :]
    sub = lax.broadcasted_iota(jnp.int32, (SUBLANES, D_MODEL), 0)

    def project(s):
        h = _rms(x_ref[s], g_ref[...], NORM_EPS).astype(BF16)
        h = jnp.dot(perm_ref[...], h, preferred_element_type=F32).astype(BF16)
        gate = _gelu_tanh(jnp.dot(h, wy_ref[...], preferred_element_type=F32))
        xp = jnp.dot(h, wx_ref[...], preferred_element_type=F32)
        return gate, xp

    def conv_and_gates(s, xp):
        lead = []
        for k in range(CONV_WIDTH - 1):
            cur = group(xp, REC_STEPS - (CONV_WIDTH - 1) + k)
            prev_last = tail_ref[s, (k + 1) * SUBLANES - 1:(k + 1) * SUBLANES, :]
            lead.append(jnp.where(sub == 0, prev_last, pltpu.roll(cur, 1, 0)))
        tail_ref[s] = xp[TM_REC - n_tail:, :]
        ext = jnp.concatenate(lead + [xp], axis=0)
        cw = cw_ref[...]
        xb = cb_ref[...]
        for j in range(CONV_WIDTH):
            xb = xb + ext[j * SUBLANES:j * SUBLANES + TM_REC, :] * cw[j:j + 1, :]
        xb16 = xb.astype(BF16)

        def block_diag(w_ref):
            return jnp.concatenate(
                [jnp.dot(xb16[:, n * RG_BLOCK:(n + 1) * RG_BLOCK], w_ref[n],
                         preferred_element_type=F32) for n in range(RG_HEADS)], axis=1)

        return xb, block_diag(wa_ref), block_diag(wi_ref)

    def recur(s, gate, xb, ra, ri):
        r = jax.nn.sigmoid(ra + ba_ref[...])
        i = jax.nn.sigmoid(ri + bi_ref[...])
        neg_lam = -lam_ref[...]
        softplus = jnp.maximum(neg_lam, 0.0) + jnp.log1p(jnp.exp(-jnp.abs(neg_lam)))
        rate = (-RG_C * math.log2(math.e)) * softplus
        a = jnp.exp2(r * rate)
        y = 1.0 - a * a
        u = (y * lax.rsqrt(jnp.maximum(y, jnp.finfo(F32).tiny))) * (i * xb)
        hz, pz = [group(u, 0)], [group(a, 0)]
        for k in range(1, REC_STEPS):
            a_k = group(a, k)
            hz.append(a_k * hz[-1] + group(u, k))
            pz.append(a_k * pz[-1])
        start = h_ref[s]
        starts = []
        for b in range(SUBLANES):
            starts.append(start)
            start = hz[-1][b:b + 1, :] + pz[-1][b:b + 1, :] * start
        h_ref[s] = start
        starts = jnp.concatenate(starts, axis=0)
        y = jnp.concatenate([(hz[k] + pz[k] * starts) * group(gate, k) for k in range(REC_STEPS)],
                            axis=0).astype(BF16)
        out_ref[s] = jnp.dot(unperm_ref[...], y, preferred_element_type=F32).astype(BF16)

    streams = range(REC_STREAMS)
    proj = [project(s) for s in streams]
    mid = [conv_and_gates(s, proj[s][1]) for s in streams]
    for s in streams:
        recur(s, proj[s][0], *mid[s])


def _rglru(x, g, wx, wy, cw, cb, wa, ba, wi, bi, lam):
    batch, seq, _ = x.shape
    src = (jnp.arange(TM_REC) % SUBLANES) * REC_STEPS + jnp.arange(TM_REC) // SUBLANES
    perm = (src[:, None] == jnp.arange(TM_REC)[None, :]).astype(BF16)
    c2 = lambda b, t: (0, 0)
    c3 = lambda b, t: (0, 0, 0)
    return pl.pallas_call(
        _rglru_kernel,
        grid=(batch // REC_STREAMS, seq // TM_REC),
        in_specs=[
            pl.BlockSpec((REC_STREAMS, TM_REC, D_MODEL), lambda b, t: (b, t, 0)),
            pl.BlockSpec((1, D_MODEL), c2),
            pl.BlockSpec((TM_REC, TM_REC), c2),
            pl.BlockSpec((TM_REC, TM_REC), c2),
            pl.BlockSpec((D_MODEL, D_MODEL), c2),
            pl.BlockSpec((D_MODEL, D_MODEL), c2),
            pl.BlockSpec((CONV_WIDTH, D_MODEL), c2),
            pl.BlockSpec((1, D_MODEL), c2),
            pl.BlockSpec((RG_HEADS, RG_BLOCK, RG_BLOCK), c3),
            pl.BlockSpec((1, D_MODEL), c2),
            pl.BlockSpec((RG_HEADS, RG_BLOCK, RG_BLOCK), c3),
            pl.BlockSpec((1, D_MODEL), c2),
            pl.BlockSpec((1, D_MODEL), c2),
        ],
        out_specs=pl.BlockSpec((REC_STREAMS, TM_REC, D_MODEL), lambda b, t: (b, t, 0)),
        out_shape=jax.ShapeDtypeStruct((batch, seq, D_MODEL), BF16),
        scratch_shapes=[
            pltpu.VMEM((REC_STREAMS, (CONV_WIDTH - 1) * SUBLANES, D_MODEL), F32),
            pltpu.VMEM((REC_STREAMS, 1, D_MODEL), F32),
        ],
        compiler_params=pltpu.CompilerParams(
            dimension_semantics=("arbitrary", "arbitrary"), vmem_limit_bytes=VMEM_LIMIT),
        name="rglru",
    )(x, g, perm, perm.T, wx, wy, cw, cb, wa, ba, wi, bi, lam)


def _rope_tables(seq):
    half = HEAD_DIM // 2
    inv_freq = 1.0 / (ROPE_THETA ** (jnp.arange(0, HEAD_DIM, 2, dtype=F32) / HEAD_DIM))
    ang = jnp.arange(seq, dtype=jnp.int32).astype(F32)[:, None] * inv_freq[None, :]
    cos = jnp.tile(jnp.cos(ang), (1, LANES // half))
    sin = jnp.tile(jnp.sin(ang), (1, LANES // half))
    sign = jnp.where(jnp.arange(LANES) < LANES // 2, -1.0, 1.0).astype(F32)
    return cos, sin * sign[None, :]


def _permute_heads(w):
    d_in = w.shape[0]
    half = HEAD_DIM // 2
    return w.reshape(d_in, ATTN_HEADS, 2, 2, half).transpose(0, 1, 3, 2, 4).reshape(d_in, -1)


def kernel(x, mix_norm_g, mlp_norm_g, attn_w_qkv, attn_w_o, attn_lq1, attn_lk1, attn_lq2, attn_lk2,
           attn_subln_g, rec_w_x, rec_w_y, rec_conv_w, rec_conv_b, rec_w_a, rec_b_a, rec_w_i, rec_b_i,
           rec_lambda, rec_w_o, mlp_w1, mlp_w2, final_norm_g):
    batch, seq, d = x.shape
    x2d = x.reshape(batch * seq, d)
    row = lambda v: v.reshape(1, -1)

    lambda_init = 0.8 - 0.6 * math.exp(-0.3 * 0)
    w_qkv = attn_w_qkv[0]
    wq = _permute_heads(w_qkv[:, :d]).astype(BF16)
    wk = _permute_heads(w_qkv[:, d:2 * d]).astype(BF16)
    wvt = w_qkv[:, 2 * d:].T.astype(BF16)
    cos, sin = _rope_tables(seq)
    q, k, vt = _qkv_rope(x2d, row(mix_norm_g[0]), wq, wk, wvt, cos, sin, batch, seq)
    lqk = jnp.stack([attn_lq1[0], attn_lk1[0], attn_lq2[0], attn_lk2[0]])
    o = _diff_attn(lqk, attn_subln_g[0].reshape(V_DIM, 1),
                   q.reshape(batch, seq, d), k.reshape(batch, seq, d), vt, lambda_init)
    x2d = _proj_mlp(x2d, o.reshape(batch * seq, d), attn_w_o[0].astype(BF16), row(mlp_norm_g[0]),
                    mlp_w1[0].astype(BF16), mlp_w2[0].astype(BF16), row(final_norm_g), False)

    gated = _rglru(x2d.reshape(batch, seq, d), row(mix_norm_g[1]),
                   rec_w_x[0].astype(BF16), rec_w_y[0].astype(BF16), rec_conv_w[0], row(rec_conv_b[0]),
                   rec_w_a[0].astype(BF16), row(rec_b_a[0]), rec_w_i[0].astype(BF16), row(rec_b_i[0]),
                   row(rec_lambda[0]))
    out = _proj_mlp(x2d, gated.reshape(batch * seq, d), rec_w_o[0].astype(BF16), row(mlp_norm_g[1]),
                    mlp_w1[1].astype(BF16), mlp_w2[1].astype(BF16), row(final_norm_g), True)
    return out.reshape(batch, seq, d)
```

```python
import functools
import math

import jax
import jax.numpy as jnp
from jax import lax
from jax.experimental import pallas as pl
from jax.experimental.pallas import tpu as pltpu

F32 = jnp.float32
BF16 = jnp.bfloat16

D_MODEL = 1024
CHUNK = 64
ATTN_HEADS = 8
HEAD_DIM = 64
V_DIM = 128
VT_ROWS = V_DIM + 16
ROPE_THETA = 10000.0
RG_BLOCK = 256
RG_HEADS = 4
CONV_WIDTH = 4
RG_C = 8.0
D_FF = 4 * D_MODEL
NORM_EPS = 1e-6
SUBLN_EPS = 1e-5

LANES = 128
SUBLANES = 8
VMEM_LIMIT = 56 * 1024 * 1024

TM_QKV = 1024
TQ = 512
TK = TQ
N_STREAMS = 4
ROW_CHUNK = 64
BF16_ROWS = 2 * SUBLANES
assert LANES == 2 * CHUNK and TK == TQ
TM_MLP = 512
FF_CHUNK = 1024
TM_REC = 256
REC_STEPS = TM_REC // SUBLANES
REC_STREAMS = 4

NT = (((1,), (1,)), ((), ()))
Q_SCALE = (HEAD_DIM ** -0.5) * math.log2(math.e)


def _rms(x, g, eps):
    ms = jnp.mean(x * x, axis=-1, keepdims=True)
    return (x * lax.rsqrt(ms + eps)) * g


def _gelu_tanh(x):
    c0 = math.sqrt(2.0 / math.pi)
    inner = x * (c0 + (c0 * 0.044715) * (x * x))
    return (0.5 * x) * (1.0 + jnp.tanh(inner))


def _qkv_kernel(x_ref, g_ref, wqt_ref, wk_ref, wvt_ref, cos_ref, sin_ref, cos_t_ref, sin_t_ref,
                qt_ref, k_ref, vt_ref):
    y = _rms(x_ref[...], g_ref[...], NORM_EPS).astype(BF16)
    half = LANES // 2
    qt = lax.dot_general(wqt_ref[...], y, NT, preferred_element_type=F32)
    cos_t = cos_t_ref[...]
    sin_t = sin_t_ref[...]
    for h in range(ATTN_HEADS):
        t = qt[h * LANES:(h + 1) * LANES, :]
        partner = jnp.concatenate([t[half:, :], t[:half, :]], axis=0)
        qt_ref[0, h] = ((t * cos_t + partner * sin_t) * Q_SCALE).astype(BF16)
    cos = cos_ref[...]
    sin = sin_ref[...]
    k = jnp.dot(y, wk_ref[...], preferred_element_type=F32)
    for h in range(ATTN_HEADS):
        sl = slice(h * LANES, (h + 1) * LANES)
        t = k[:, sl]
        k_ref[:, sl] = (t * cos + pltpu.roll(t, half, 1) * sin).astype(BF16)
    vt = lax.dot_general(wvt_ref[...], y, NT, preferred_element_type=F32)
    for h in range(ATTN_HEADS):
        for t in range(TM_QKV // TK):
            vt_ref[0, h, t, :V_DIM, :] = vt[h * V_DIM:(h + 1) * V_DIM, t * TK:(t + 1) * TK].astype(BF16)
            vt_ref[0, h, t, V_DIM:, :] = jnp.ones((VT_ROWS - V_DIM, TK), BF16)


def _qkv_rope(x2d, g, wqt, wk, wvt, cos, sin, batch, seq):
    n_tok = x2d.shape[0]
    tiles_per_seq = seq // TM_QKV
    const = lambda i: (0, 0)
    return pl.pallas_call(
        _qkv_kernel,
        grid=(n_tok // TM_QKV,),
        in_specs=[
            pl.BlockSpec((TM_QKV, D_MODEL), lambda i: (i, 0)),
            pl.BlockSpec((1, D_MODEL), const),
            pl.BlockSpec((D_MODEL, D_MODEL), const),
            pl.BlockSpec((D_MODEL, D_MODEL), const),
            pl.BlockSpec((D_MODEL, D_MODEL), const),
            pl.BlockSpec((TM_QKV, LANES), lambda i: (i % tiles_per_seq, 0)),
            pl.BlockSpec((TM_QKV, LANES), lambda i: (i % tiles_per_seq, 0)),
            pl.BlockSpec((LANES, TM_QKV), lambda i: (0, i % tiles_per_seq)),
            pl.BlockSpec((LANES, TM_QKV), lambda i: (0, i % tiles_per_seq)),
        ],
        out_specs=[
            pl.BlockSpec((1, ATTN_HEADS, LANES, TM_QKV),
                         lambda i: (i // tiles_per_seq, 0, 0, i % tiles_per_seq)),
            pl.BlockSpec((TM_QKV, D_MODEL), lambda i: (i, 0)),
            pl.BlockSpec((1, ATTN_HEADS, TM_QKV // TK, VT_ROWS, TK),
                         lambda i: (i // tiles_per_seq, 0, i % tiles_per_seq, 0, 0)),
        ],
        out_shape=[
            jax.ShapeDtypeStruct((batch, ATTN_HEADS, LANES, seq), BF16),
            jax.ShapeDtypeStruct((n_tok, D_MODEL), BF16),
            jax.ShapeDtypeStruct((batch, ATTN_HEADS, seq // TK, VT_ROWS, TK), BF16),
        ],
        compiler_params=pltpu.CompilerParams(
            dimension_semantics=("arbitrary",), vmem_limit_bytes=VMEM_LIMIT),
        name="qkv_rope",
    )(x2d, g, wqt, wk, wvt, cos, sin, cos.T, sin.T)


def _attn_kernel(lqk_ref, g_ref, q_ref, k_ref, vt_ref, o_ref, m_ref, l_ref, acc_ref, alpha_ref, p_ref,
                 *, lambda_init):
    qi = pl.program_id(2)

    feature = lax.broadcasted_iota(jnp.int32, (LANES, TQ), 0)
    is_map1 = (feature // (HEAD_DIM // 2)) % 2 == 0
    keep1 = jnp.where(is_map1, 1.0, 0.0).astype(BF16)
    keep2 = jnp.where(is_map1, 0.0, 1.0).astype(BF16)
    q_st = []
    for s in range(N_STREAMS):
        qt = q_ref[0, s]
        q_st.append(jnp.concatenate([qt * keep1, qt * keep2], axis=1))

    def score_matmul(s, j):
        rows = pl.ds(pl.multiple_of(j * TK, TK), TK)
        return jnp.dot(k_ref[0, rows, s * LANES:(s + 1) * LANES], q_st[s],
                       preferred_element_type=F32)

    def value_matmul(s, j):
        return jnp.dot(vt_ref[0, s, j], p_ref[s], preferred_element_type=F32)

    def accumulate(s, pv):
        acc_ref[s] = alpha_ref[s] * acc_ref[s] + pv[:V_DIM, :]
        l_ref[s] = alpha_ref[s] * l_ref[s] + pv[V_DIM:V_DIM + 1, :]

    def matmuls(j, pending):
        scores, pv = [], {}
        for s in range(N_STREAMS):
            scores.append(score_matmul(s, j))
            if pending is not None and s >= 1:
                pv[s - 1] = value_matmul(s - 1, pending)
        if pending is not None:
            pv[N_STREAMS - 1] = value_matmul(N_STREAMS - 1, pending)
        return scores, pv

    upper_half = lax.broadcasted_iota(jnp.int32, (BF16_ROWS, LANES), 1) >= CHUNK

    def key_tile(j, own, pending):
        scores, pv = matmuls(j, pending)
        width = LANES if own else 2 * TQ
        lane_blocks = [slice(c, c + width) for c in range(0, 2 * TQ, width)]
        cache = {}

        def block(s, g, cols):
            if (s, g, cols.start) not in cache:
                blk = scores[s][g:g + BF16_ROWS, cols]
                if own:
                    key_chunk = g // CHUNK
                    qry_chunk = (cols.start % TQ) // CHUNK
                    if qry_chunk + 1 < key_chunk:
                        blk = None
                    elif qry_chunk < key_chunk:
                        blk = jnp.where(upper_half, blk, -jnp.inf)
                cache[s, g, cols.start] = blk
            return cache[s, g, cols.start]

        def row_groups(c):
            return range(c * ROW_CHUNK, (c + 1) * ROW_CHUNK, BF16_ROWS)

        m_new = {}

        def max_pass(s):
            if pending is not None:
                accumulate(s, pv[s])
            mx = {}
            for c in range(TK // ROW_CHUNK):
                for g in row_groups(c):
                    for cols in lane_blocks:
                        blk = block(s, g, cols)
                        if blk is not None:
                            seen = mx.get(cols.start)
                            mx[cols.start] = blk if seen is None else jnp.maximum(seen, blk)
                yield
            m_old = m_ref[s]
            col_max = jnp.concatenate(
                [jnp.max(mx[cols.start], axis=0, keepdims=True) for cols in lane_blocks], axis=1)
            m_new[s] = jnp.maximum(m_old, col_max)
            alpha_ref[s] = jnp.exp2(m_old - m_new[s])
            m_ref[s] = m_new[s]

        def exp_pass(s):
            for c in range(TK // ROW_CHUNK):
                for g in row_groups(c):
                    for cols in lane_blocks:
                        blk = block(s, g, cols)
                        if blk is None:
                            p = jnp.zeros((BF16_ROWS, width), BF16)
                        else:
                            p = jnp.exp2(blk - m_new[s][:, cols]).astype(BF16)
                        p_ref[s, g:g + BF16_ROWS, cols] = p
                yield

        for _ in max_pass(0):
            pass
        for s in range(N_STREAMS):
            ahead = max_pass(s + 1) if s + 1 < N_STREAMS else iter(())
            for _ in exp_pass(s):
                next(ahead, None)
            for _ in ahead:
                pass

    m_ref[...] = jnp.full(m_ref.shape, -1e30, F32)
    l_ref[...] = jnp.zeros(l_ref.shape, F32)
    acc_ref[...] = jnp.zeros(acc_ref.shape, F32)
    key_tile(qi, True, None)

    def body(j, pending):
        key_tile(j, False, pending)
        return j

    pending = lax.fori_loop(0, qi, body, qi)
    for s in range(N_STREAMS):
        accumulate(s, value_matmul(s, pending))

    lqk = lqk_ref[...]
    lam = (jnp.exp(jnp.sum(lqk[0:1] * lqk[1:2], axis=1, keepdims=True))
           - jnp.exp(jnp.sum(lqk[2:3] * lqk[3:4], axis=1, keepdims=True))
           + lambda_init)
    for s in range(N_STREAMS):
        acc = acc_ref[s]
        l = l_ref[s]
        o = acc[:, :TQ] / l[:, :TQ] - lam * (acc[:, TQ:] / l[:, TQ:])
        ms = jnp.mean(o * o, axis=0, keepdims=True)
        y = ((o * lax.rsqrt(ms + SUBLN_EPS)) * g_ref[...]) * (1.0 - lambda_init)
        o_ref[0, :, s * LANES:(s + 1) * LANES] = y.T.astype(BF16)


def _diff_attn(lqk, g_col, qt, k, vt, lambda_init):
    batch, seq, _ = k.shape
    width = N_STREAMS * LANES
    kernel = functools.partial(_attn_kernel, lambda_init=lambda_init)
    return pl.pallas_call(
        kernel,
        grid=(batch, ATTN_HEADS // N_STREAMS, seq // TQ),
        in_specs=[
            pl.BlockSpec((4, HEAD_DIM), lambda b, h, i: (0, 0)),
            pl.BlockSpec((V_DIM, 1), lambda b, h, i: (0, 0)),
            pl.BlockSpec((1, N_STREAMS, LANES, TQ), lambda b, h, i: (b, h, 0, i)),
            pl.BlockSpec((1, seq, width), lambda b, h, i: (b, 0, h)),
            pl.BlockSpec((1, N_STREAMS, seq // TK, VT_ROWS, TK), lambda b, h, i: (b, h, 0, 0, 0)),
        ],
        out_specs=pl.BlockSpec((1, TQ, width), lambda b, h, i: (b, i, h)),
        out_shape=jax.ShapeDtypeStruct((batch, seq, ATTN_HEADS * V_DIM), BF16),
        scratch_shapes=[
            pltpu.VMEM((N_STREAMS, 1, 2 * TQ), F32),
            pltpu.VMEM((N_STREAMS, 1, 2 * TQ), F32),
            pltpu.VMEM((N_STREAMS, V_DIM, 2 * TQ), F32),
            pltpu.VMEM((N_STREAMS, 1, 2 * TQ), F32),
            pltpu.VMEM((N_STREAMS, TK, 2 * TQ), BF16),
        ],
        compiler_params=pltpu.CompilerParams(
            dimension_semantics=("arbitrary", "arbitrary", "arbitrary"),
            vmem_limit_bytes=VMEM_LIMIT),
        name="diff_attn",
    )(lqk, g_col, qt, k, vt)


def _proj_mlp_kernel(x_ref, a_ref, wo_ref, g_ref, w1_ref, w2_ref, gf_ref, out_ref, *, final_norm):
    x1 = x_ref[...] + jnp.dot(a_ref[...], wo_ref[...], preferred_element_type=F32)
    h = _rms(x1, g_ref[...], NORM_EPS).astype(BF16)
    acc = x1
    for c in range(D_FF // FF_CHUNK):
        sl = slice(c * FF_CHUNK, (c + 1) * FF_CHUNK)
        u = jnp.dot(h, w1_ref[:, sl], preferred_element_type=F32)
        u = jnp.square(jnp.maximum(u, 0.0)).astype(BF16)
        acc = acc + jnp.dot(u, w2_ref[sl, :], preferred_element_type=F32)
    if final_norm:
        acc = _rms(acc, gf_ref[...], NORM_EPS)
    out_ref[...] = acc


def _proj_mlp(x2d, a2d, wo, g, w1, w2, gf, final_norm):
    n_tok = x2d.shape[0]
    const = lambda i: (0, 0)
    resident = pl.Buffered(1)
    kernel = functools.partial(_proj_mlp_kernel, final_norm=final_norm)
    return pl.pallas_call(
        kernel,
        grid=(n_tok // TM_MLP,),
        in_specs=[
            pl.BlockSpec((TM_MLP, D_MODEL), lambda i: (i, 0)),
            pl.BlockSpec((TM_MLP, D_MODEL), lambda i: (i, 0)),
            pl.BlockSpec((D_MODEL, D_MODEL), const, pipeline_mode=resident),
            pl.BlockSpec((1, D_MODEL), const),
            pl.BlockSpec((D_MODEL, D_FF), const, pipeline_mode=resident),
            pl.BlockSpec((D_FF, D_MODEL), const, pipeline_mode=resident),
            pl.BlockSpec((1, D_MODEL), const),
        ],
        out_specs=pl.BlockSpec((TM_MLP, D_MODEL), lambda i: (i, 0)),
        out_shape=jax.ShapeDtypeStruct((n_tok, D_MODEL), F32),
        compiler_params=pltpu.CompilerParams(
            dimension_semantics=("arbitrary",), vmem_limit_bytes=VMEM_LIMIT),
        name="proj_mlp_final" if final_norm else "proj_mlp",
    )(x2d, a2d, wo, g, w1, w2, gf)


def _rglru_kernel(x_ref, g_ref, perm_ref, unperm_ref, wx_ref, wy_ref, cw_ref, cb_ref, wa_ref, ba_ref,
                  wi_ref, bi_ref, lam_ref, out_ref, tail_ref, h_ref):
    t = pl.program_id(1)
    n_tail = (CONV_WIDTH - 1) * SUBLANES

    @pl.when(t == 0)
    def _():
        tail_ref[...] = jnp.zeros(tail_ref.shape, F32)
        h_ref[...] = jnp.zeros(h_ref.shape, F32)

    group = lambda v, r: v[r * SUBLANES:(r + 1) * SUBLANES, :]
    sub = lax.broadcasted_iota(jnp.int32, (SUBLANES, D_MODEL), 0)

    def project(s):
        h = _rms(x_ref[s], g_ref[...], NORM_EPS).astype(BF16)
        h = jnp.dot(perm_ref[...], h, preferred_element_type=F32).astype(BF16)
        gate = _gelu_tanh(jnp.dot(h, wy_ref[...], preferred_element_type=F32))
        xp = jnp.dot(h, wx_ref[...], preferred_element_type=F32)
        return gate, xp

    def conv_and_gates(s, xp):
        lead = []
        for k in range(CONV_WIDTH - 1):
            cur = group(xp, REC_STEPS - (CONV_WIDTH - 1) + k)
            prev_last = tail_ref[s, (k + 1) * SUBLANES - 1:(k + 1) * SUBLANES, :]
            lead.append(jnp.where(sub == 0, prev_last, pltpu.roll(cur, 1, 0)))
        tail_ref[s] = xp[TM_REC - n_tail:, :]
        ext = jnp.concatenate(lead + [xp], axis=0)
        cw = cw_ref[...]
        xb = cb_ref[...]
        for j in range(CONV_WIDTH):
            xb = xb + ext[j * SUBLANES:j * SUBLANES + TM_REC, :] * cw[j:j + 1, :]
        xb16 = xb.astype(BF16)

        def block_diag(w_ref):
            return jnp.concatenate(
                [jnp.dot(xb16[:, n * RG_BLOCK:(n + 1) * RG_BLOCK], w_ref[n],
                         preferred_element_type=F32) for n in range(RG_HEADS)], axis=1)

        return xb, block_diag(wa_ref), block_diag(wi_ref)

    def recur(s, gate, xb, ra, ri):
        r = jax.nn.sigmoid(ra + ba_ref[...])
        i = jax.nn.sigmoid(ri + bi_ref[...])
        neg_lam = -lam_ref[...]
        softplus = jnp.maximum(neg_lam, 0.0) + jnp.log1p(jnp.exp(-jnp.abs(neg_lam)))
        rate = (-RG_C * math.log2(math.e)) * softplus
        a = jnp.exp2(r * rate)
        y = 1.0 - a * a
        u = (y * lax.rsqrt(jnp.maximum(y, jnp.finfo(F32).tiny))) * (i * xb)
        hz, pz = [group(u, 0)], [group(a, 0)]
        for k in range(1, REC_STEPS):
            a_k = group(a, k)
            hz.append(a_k * hz[-1] + group(u, k))
            pz.append(a_k * pz[-1])
        start = h_ref[s]
        starts = []
        for b in range(SUBLANES):
            starts.append(start)
            start = hz[-1][b:b + 1, :] + pz[-1][b:b + 1, :] * start
        h_ref[s] = start
        starts = jnp.concatenate(starts, axis=0)
        y = jnp.concatenate([(hz[k] + pz[k] * starts) * group(gate, k) for k in range(REC_STEPS)],
                            axis=0).astype(BF16)
        out_ref[s] = jnp.dot(unperm_ref[...], y, preferred_element_type=F32).astype(BF16)

    streams = range(REC_STREAMS)
    proj = [project(s) for s in streams]
    mid = [conv_and_gates(s, proj[s][1]) for s in streams]
    for s in streams:
        recur(s, proj[s][0], *mid[s])


def _rglru(x, g, wx, wy, cw, cb, wa, ba, wi, bi, lam):
    batch, seq, _ = x.shape
    src = (jnp.arange(TM_REC) % SUBLANES) * REC_STEPS + jnp.arange(TM_REC) // SUBLANES
    perm = (src[:, None] == jnp.arange(TM_REC)[None, :]).astype(BF16)
    c2 = lambda b, t: (0, 0)
    c3 = lambda b, t: (0, 0, 0)
    return pl.pallas_call(
        _rglru_kernel,
        grid=(batch // REC_STREAMS, seq // TM_REC),
        in_specs=[
            pl.BlockSpec((REC_STREAMS, TM_REC, D_MODEL), lambda b, t: (b, t, 0)),
            pl.BlockSpec((1, D_MODEL), c2),
            pl.BlockSpec((TM_REC, TM_REC), c2),
            pl.BlockSpec((TM_REC, TM_REC), c2),
            pl.BlockSpec((D_MODEL, D_MODEL), c2),
            pl.BlockSpec((D_MODEL, D_MODEL), c2),
            pl.BlockSpec((CONV_WIDTH, D_MODEL), c2),
            pl.BlockSpec((1, D_MODEL), c2),
            pl.BlockSpec((RG_HEADS, RG_BLOCK, RG_BLOCK), c3),
            pl.BlockSpec((1, D_MODEL), c2),
            pl.BlockSpec((RG_HEADS, RG_BLOCK, RG_BLOCK), c3),
            pl.BlockSpec((1, D_MODEL), c2),
            pl.BlockSpec((1, D_MODEL), c2),
        ],
        out_specs=pl.BlockSpec((REC_STREAMS, TM_REC, D_MODEL), lambda b, t: (b, t, 0)),
        out_shape=jax.ShapeDtypeStruct((batch, seq, D_MODEL), BF16),
        scratch_shapes=[
            pltpu.VMEM((REC_STREAMS, (CONV_WIDTH - 1) * SUBLANES, D_MODEL), F32),
            pltpu.VMEM((REC_STREAMS, 1, D_MODEL), F32),
        ],
        compiler_params=pltpu.CompilerParams(
            dimension_semantics=("arbitrary", "arbitrary"), vmem_limit_bytes=VMEM_LIMIT),
        name="rglru",
    )(x, g, perm, perm.T, wx, wy, cw, cb, wa, ba, wi, bi, lam)


def _rope_tables(seq):
    half = HEAD_DIM // 2
    inv_freq = 1.0 / (ROPE_THETA ** (jnp.arange(0, HEAD_DIM, 2, dtype=F32) / HEAD_DIM))
    ang = jnp.arange(seq, dtype=jnp.int32).astype(F32)[:, None] * inv_freq[None, :]
    cos = jnp.tile(jnp.cos(ang), (1, LANES // half))
    sin = jnp.tile(jnp.sin(ang), (1, LANES // half))
    sign = jnp.where(jnp.arange(LANES) < LANES // 2, -1.0, 1.0).astype(F32)
    return cos, sin * sign[None, :]


def _permute_heads(w):
    d_in = w.shape[0]
    half = HEAD_DIM // 2
    return w.reshape(d_in, ATTN_HEADS, 2, 2, half).transpose(0, 1, 3, 2, 4).reshape(d_in, -1)


def kernel(x, mix_norm_g, mlp_norm_g, attn_w_qkv, attn_w_o, attn_lq1, attn_lk1, attn_lq2, attn_lk2,
           attn_subln_g, rec_w_x, rec_w_y, rec_conv_w, rec_conv_b, rec_w_a, rec_b_a, rec_w_i, rec_b_i,
           rec_lambda, rec_w_o, mlp_w1, mlp_w2, final_norm_g):
    batch, seq, d = x.shape
    x2d = x.reshape(batch * seq, d)
    row = lambda v: v.reshape(1, -1)

    lambda_init = 0.8 - 0.6 * math.exp(-0.3 * 0)
    w_qkv = attn_w_qkv[0]
    wqt = _permute_heads(w_qkv[:, :d]).T.astype(BF16)
    wk = _permute_heads(w_qkv[:, d:2 * d]).astype(BF16)
    wvt = w_qkv[:, 2 * d:].T.astype(BF16)
    cos, sin = _rope_tables(seq)
    qt, k, vt = _qkv_rope(x2d, row(mix_norm_g[0]), wqt, wk, wvt, cos, sin, batch, seq)
    lqk = jnp.stack([attn_lq1[0], attn_lk1[0], attn_lq2[0], attn_lk2[0]])
    o = _diff_attn(lqk, attn_subln_g[0].reshape(V_DIM, 1),
                   qt, k.reshape(batch, seq, d), vt, lambda_init)
    x2d = _proj_mlp(x2d, o.reshape(batch * seq, d), attn_w_o[0].astype(BF16), row(mlp_norm_g[0]),
                    mlp_w1[0].astype(BF16), mlp_w2[0].astype(BF16), row(final_norm_g), False)

    gated = _rglru(x2d.reshape(batch, seq, d), row(mix_norm_g[1]),
                   rec_w_x[0].astype(BF16), rec_w_y[0].astype(BF16), rec_conv_w[0], row(rec_conv_b[0]),
                   rec_w_a[0].astype(BF16), row(rec_b_a[0]), rec_w_i[0].astype(BF16), row(rec_b_i[0]),
                   row(rec_lambda[0]))
    out = _proj_mlp(x2d, gated.reshape(batch * seq, d), rec_w_o[0].astype(BF16), row(mlp_norm_g[1]),
                    mlp_w1[1].astype(BF16), mlp_w2[1].astype(BF16), row(final_norm_g), True)
    return out.reshape(batch, seq, d)
```

```python
import functools
import math

import jax
import jax.numpy as jnp
from jax import lax
from jax.experimental import pallas as pl
from jax.experimental.pallas import tpu as pltpu

F32 = jnp.float32
BF16 = jnp.bfloat16

D_MODEL = 1024
CHUNK = 64
ATTN_HEADS = 8
HEAD_DIM = 64
V_DIM = 128
VT_ROWS = V_DIM + 16
ROPE_THETA = 10000.0
RG_BLOCK = 256
RG_HEADS = 4
CONV_WIDTH = 4
RG_C = 8.0
D_FF = 4 * D_MODEL
NORM_EPS = 1e-6
SUBLN_EPS = 1e-5

LANES = 128
SUBLANES = 8
VMEM_LIMIT = 56 * 1024 * 1024

TM_QKV = 1024
TQ = 512
TK = TQ
N_STREAMS = 4
ROW_CHUNK = 64
BF16_ROWS = 2 * SUBLANES
assert LANES == 2 * CHUNK and TK == TQ
TM_MLP = 512
FF_CHUNK = 2048
TM_REC = 256
REC_STEPS = TM_REC // SUBLANES
REC_STREAMS = 4

NT = (((1,), (1,)), ((), ()))
Q_SCALE = (HEAD_DIM ** -0.5) * math.log2(math.e)


def _rms(x, g, eps):
    ms = jnp.mean(x * x, axis=-1, keepdims=True)
    return (x * lax.rsqrt(ms + eps)) * g


def _gelu_tanh(x):
    c0 = math.sqrt(2.0 / math.pi)
    inner = x * (c0 + (c0 * 0.044715) * (x * x))
    return (0.5 * x) * (1.0 + jnp.tanh(inner))


def _qkv_kernel(x_ref, g_ref, wqt_ref, wk_ref, wvt_ref, cos_ref, sin_ref, cos_t_ref, sin_t_ref,
                qt_ref, k_ref, vt_ref):
    y = _rms(x_ref[...], g_ref[...], NORM_EPS).astype(BF16)
    half = LANES // 2
    qt = lax.dot_general(wqt_ref[...], y, NT, preferred_element_type=F32)
    cos_t = cos_t_ref[...]
    sin_t = sin_t_ref[...]
    for h in range(ATTN_HEADS):
        t = qt[h * LANES:(h + 1) * LANES, :]
        partner = jnp.concatenate([t[half:, :], t[:half, :]], axis=0)
        qt_ref[0, h] = ((t * cos_t + partner * sin_t) * Q_SCALE).astype(BF16)
    cos = cos_ref[...]
    sin = sin_ref[...]
    k = jnp.dot(y, wk_ref[...], preferred_element_type=F32)
    for h in range(ATTN_HEADS):
        sl = slice(h * LANES, (h + 1) * LANES)
        t = k[:, sl]
        k_ref[:, sl] = (t * cos + pltpu.roll(t, half, 1) * sin).astype(BF16)
    vt = lax.dot_general(wvt_ref[...], y, NT, preferred_element_type=F32)
    for h in range(ATTN_HEADS):
        for t in range(TM_QKV // TK):
            vt_ref[0, h, t, :V_DIM, :] = vt[h * V_DIM:(h + 1) * V_DIM, t * TK:(t + 1) * TK].astype(BF16)
            vt_ref[0, h, t, V_DIM:, :] = jnp.ones((VT_ROWS - V_DIM, TK), BF16)


def _qkv_rope(x2d, g, wqt, wk, wvt, cos, sin, batch, seq):
    n_tok = x2d.shape[0]
    tiles_per_seq = seq // TM_QKV
    const = lambda i: (0, 0)
    return pl.pallas_call(
        _qkv_kernel,
        grid=(n_tok // TM_QKV,),
        in_specs=[
            pl.BlockSpec((TM_QKV, D_MODEL), lambda i: (i, 0)),
            pl.BlockSpec((1, D_MODEL), const),
            pl.BlockSpec((D_MODEL, D_MODEL), const),
            pl.BlockSpec((D_MODEL, D_MODEL), const),
            pl.BlockSpec((D_MODEL, D_MODEL), const),
            pl.BlockSpec((TM_QKV, LANES), lambda i: (i % tiles_per_seq, 0)),
            pl.BlockSpec((TM_QKV, LANES), lambda i: (i % tiles_per_seq, 0)),
            pl.BlockSpec((LANES, TM_QKV), lambda i: (0, i % tiles_per_seq)),
            pl.BlockSpec((LANES, TM_QKV), lambda i: (0, i % tiles_per_seq)),
        ],
        out_specs=[
            pl.BlockSpec((1, ATTN_HEADS, LANES, TM_QKV),
                         lambda i: (i // tiles_per_seq, 0, 0, i % tiles_per_seq)),
            pl.BlockSpec((TM_QKV, D_MODEL), lambda i: (i, 0)),
            pl.BlockSpec((1, ATTN_HEADS, TM_QKV // TK, VT_ROWS, TK),
                         lambda i: (i // tiles_per_seq, 0, i % tiles_per_seq, 0, 0)),
        ],
        out_shape=[
            jax.ShapeDtypeStruct((batch, ATTN_HEADS, LANES, seq), BF16),
            jax.ShapeDtypeStruct((n_tok, D_MODEL), BF16),
            jax.ShapeDtypeStruct((batch, ATTN_HEADS, seq // TK, VT_ROWS, TK), BF16),
        ],
        compiler_params=pltpu.CompilerParams(
            dimension_semantics=("arbitrary",), vmem_limit_bytes=VMEM_LIMIT),
        name="qkv_rope",
    )(x2d, g, wqt, wk, wvt, cos, sin, cos.T, sin.T)


def _attn_kernel(lqk_ref, g_ref, q_ref, k_ref, vt_ref, o_ref, m_ref, l_ref, acc_ref, alpha_ref, p_ref,
                 *, lambda_init):
    qi = pl.program_id(2)

    feature = lax.broadcasted_iota(jnp.int32, (LANES, TQ), 0)
    is_map1 = (feature // (HEAD_DIM // 2)) % 2 == 0
    keep1 = jnp.where(is_map1, 1.0, 0.0).astype(BF16)
    keep2 = jnp.where(is_map1, 0.0, 1.0).astype(BF16)
    q_st = []
    for s in range(N_STREAMS):
        qt = q_ref[0, s]
        q_st.append(jnp.concatenate([qt * keep1, qt * keep2], axis=1))

    def score_matmul(s, j, own):
        half = TK // 2

        def keys(lo, n):
            rows = pl.ds(pl.multiple_of(j * TK + lo, half), n)
            return k_ref[0, rows, s * LANES:(s + 1) * LANES]

        if not own:
            full = jnp.dot(keys(0, TK), q_st[s], preferred_element_type=F32)
            return lambda g, cols: full[g:g + BF16_ROWS, cols]
        top = jnp.dot(keys(0, half), q_st[s], preferred_element_type=F32)
        late = jnp.concatenate([q_st[s][:, m * TQ + half:(m + 1) * TQ] for m in range(2)], axis=1)
        bottom = jnp.dot(keys(half, half), late, preferred_element_type=F32)

        def read(g, cols):
            if g < half:
                return top[g:g + BF16_ROWS, cols]
            m, q0 = divmod(cols.start, TQ)
            c0 = m * half + q0 - half
            return bottom[g - half:g - half + BF16_ROWS, c0:c0 + cols.stop - cols.start]

        return read

    def value_matmul(s, j):
        return jnp.dot(vt_ref[0, s, j], p_ref[s], preferred_element_type=F32)

    def accumulate(s, pv):
        acc_ref[s] = alpha_ref[s] * acc_ref[s] + pv[:V_DIM, :]
        l_ref[s] = alpha_ref[s] * l_ref[s] + pv[V_DIM:V_DIM + 1, :]

    def matmuls(j, own, pending):
        scores, pv = [], {}
        for s in range(N_STREAMS):
            scores.append(score_matmul(s, j, own))
            if pending is not None and s >= 1:
                pv[s - 1] = value_matmul(s - 1, pending)
        if pending is not None:
            pv[N_STREAMS - 1] = value_matmul(N_STREAMS - 1, pending)
        return scores, pv

    upper_half = lax.broadcasted_iota(jnp.int32, (BF16_ROWS, LANES), 1) >= CHUNK

    def key_tile(j, own, pending):
        scores, pv = matmuls(j, own, pending)
        width = LANES if own else 2 * TQ
        lane_blocks = [slice(c, c + width) for c in range(0, 2 * TQ, width)]
        cache = {}

        def block(s, g, cols):
            if (s, g, cols.start) not in cache:
                key_chunk = g // CHUNK
                qry_chunk = (cols.start % TQ) // CHUNK
                if own and qry_chunk + 1 < key_chunk:
                    blk = None
                else:
                    blk = scores[s](g, cols)
                    if own and qry_chunk < key_chunk:
                        blk = jnp.where(upper_half, blk, -jnp.inf)
                cache[s, g, cols.start] = blk
            return cache[s, g, cols.start]

        def row_groups(c):
            return range(c * ROW_CHUNK, (c + 1) * ROW_CHUNK, BF16_ROWS)

        m_new = {}

        def max_pass(s):
            if pending is not None:
                accumulate(s, pv[s])
            mx = {}
            for c in range(TK // ROW_CHUNK):
                for g in row_groups(c):
                    for cols in lane_blocks:
                        blk = block(s, g, cols)
                        if blk is not None:
                            seen = mx.get(cols.start)
                            mx[cols.start] = blk if seen is None else jnp.maximum(seen, blk)
                yield
            m_old = m_ref[s]
            col_max = jnp.concatenate(
                [jnp.max(mx[cols.start], axis=0, keepdims=True) for cols in lane_blocks], axis=1)
            m_new[s] = jnp.maximum(m_old, col_max)
            alpha_ref[s] = jnp.exp2(m_old - m_new[s])
            m_ref[s] = m_new[s]

        def exp_pass(s):
            for c in range(TK // ROW_CHUNK):
                for g in row_groups(c):
                    for cols in lane_blocks:
                        blk = block(s, g, cols)
                        if blk is None:
                            p = jnp.zeros((BF16_ROWS, width), BF16)
                        else:
                            p = jnp.exp2(blk - m_new[s][:, cols]).astype(BF16)
                        p_ref[s, g:g + BF16_ROWS, cols] = p
                yield

        for _ in max_pass(0):
            pass
        for s in range(N_STREAMS):
            ahead = max_pass(s + 1) if s + 1 < N_STREAMS else iter(())
            for _ in exp_pass(s):
                next(ahead, None)
            for _ in ahead:
                pass

    m_ref[...] = jnp.full(m_ref.shape, -1e30, F32)
    l_ref[...] = jnp.zeros(l_ref.shape, F32)
    acc_ref[...] = jnp.zeros(acc_ref.shape, F32)
    key_tile(qi, True, None)

    def body(j, pending):
        key_tile(j, False, pending)
        return j

    pending = lax.fori_loop(0, qi, body, qi)
    for s in range(N_STREAMS):
        accumulate(s, value_matmul(s, pending))

    lqk = lqk_ref[...]
    lam = (jnp.exp(jnp.sum(lqk[0:1] * lqk[1:2], axis=1, keepdims=True))
           - jnp.exp(jnp.sum(lqk[2:3] * lqk[3:4], axis=1, keepdims=True))
           + lambda_init)
    for s in range(N_STREAMS):
        acc = acc_ref[s]
        l = l_ref[s]
        o = acc[:, :TQ] / l[:, :TQ] - lam * (acc[:, TQ:] / l[:, TQ:])
        ms = jnp.mean(o * o, axis=0, keepdims=True)
        y = ((o * lax.rsqrt(ms + SUBLN_EPS)) * g_ref[...]) * (1.0 - lambda_init)
        o_ref[0, :, s * LANES:(s + 1) * LANES] = y.T.astype(BF16)


def _diff_attn(lqk, g_col, qt, k, vt, lambda_init):
    batch, seq, _ = k.shape
    width = N_STREAMS * LANES
    kernel = functools.partial(_attn_kernel, lambda_init=lambda_init)
    return pl.pallas_call(
        kernel,
        grid=(batch, ATTN_HEADS // N_STREAMS, seq // TQ),
        in_specs=[
            pl.BlockSpec((4, HEAD_DIM), lambda b, h, i: (0, 0)),
            pl.BlockSpec((V_DIM, 1), lambda b, h, i: (0, 0)),
            pl.BlockSpec((1, N_STREAMS, LANES, TQ), lambda b, h, i: (b, h, 0, i)),
            pl.BlockSpec((1, seq, width), lambda b, h, i: (b, 0, h)),
            pl.BlockSpec((1, N_STREAMS, seq // TK, VT_ROWS, TK), lambda b, h, i: (b, h, 0, 0, 0)),
        ],
        out_specs=pl.BlockSpec((1, TQ, width), lambda b, h, i: (b, i, h)),
        out_shape=jax.ShapeDtypeStruct((batch, seq, ATTN_HEADS * V_DIM), BF16),
        scratch_shapes=[
            pltpu.VMEM((N_STREAMS, 1, 2 * TQ), F32),
            pltpu.VMEM((N_STREAMS, 1, 2 * TQ), F32),
            pltpu.VMEM((N_STREAMS, V_DIM, 2 * TQ), F32),
            pltpu.VMEM((N_STREAMS, 1, 2 * TQ), F32),
            pltpu.VMEM((N_STREAMS, TK, 2 * TQ), BF16),
        ],
        compiler_params=pltpu.CompilerParams(
            dimension_semantics=("arbitrary", "arbitrary", "arbitrary"),
            vmem_limit_bytes=VMEM_LIMIT),
        name="diff_attn",
    )(lqk, g_col, qt, k, vt)


def _proj_mlp_kernel(x_ref, a_ref, wo_ref, g_ref, w1_ref, w2_ref, gf_ref, out_ref, *, final_norm):
    x1 = x_ref[...] + jnp.dot(a_ref[...], wo_ref[...], preferred_element_type=F32)
    h = _rms(x1, g_ref[...], NORM_EPS).astype(BF16)
    acc = x1
    for c in range(D_FF // FF_CHUNK):
        sl = slice(c * FF_CHUNK, (c + 1) * FF_CHUNK)
        u = jnp.dot(h, w1_ref[:, sl], preferred_element_type=F32)
        u = jnp.square(jnp.maximum(u, 0.0)).astype(BF16)
        acc = acc + jnp.dot(u, w2_ref[sl, :], preferred_element_type=F32)
    if final_norm:
        acc = _rms(acc, gf_ref[...], NORM_EPS)
    out_ref[...] = acc


def _proj_mlp(x2d, a2d, wo, g, w1, w2, gf, final_norm):
    n_tok = x2d.shape[0]
    const = lambda i: (0, 0)
    resident = pl.Buffered(1)
    kernel = functools.partial(_proj_mlp_kernel, final_norm=final_norm)
    return pl.pallas_call(
        kernel,
        grid=(n_tok // TM_MLP,),
        in_specs=[
            pl.BlockSpec((TM_MLP, D_MODEL), lambda i: (i, 0)),
            pl.BlockSpec((TM_MLP, D_MODEL), lambda i: (i, 0)),
            pl.BlockSpec((D_MODEL, D_MODEL), const, pipeline_mode=resident),
            pl.BlockSpec((1, D_MODEL), const),
            pl.BlockSpec((D_MODEL, D_FF), const, pipeline_mode=resident),
            pl.BlockSpec((D_FF, D_MODEL), const, pipeline_mode=resident),
            pl.BlockSpec((1, D_MODEL), const),
        ],
        out_specs=pl.BlockSpec((TM_MLP, D_MODEL), lambda i: (i, 0)),
        out_shape=jax.ShapeDtypeStruct((n_tok, D_MODEL), F32),
        compiler_params=pltpu.CompilerParams(
            dimension_semantics=("arbitrary",), vmem_limit_bytes=VMEM_LIMIT),
        name="proj_mlp_final" if final_norm else "proj_mlp",
    )(x2d, a2d, wo, g, w1, w2, gf)


def _rglru_kernel(x_ref, g_ref, perm_ref, unperm_ref, wx_ref, wy_ref, cw_ref, cb_ref, wa_ref, ba_ref,
                  wi_ref, bi_ref, lam_ref, out_ref, tail_ref, h_ref):
    t = pl.program_id(1)
    n_tail = (CONV_WIDTH - 1) * SUBLANES

    @pl.when(t == 0)
    def _():
        tail_ref[...] = jnp.zeros(tail_ref.shape, F32)
        h_ref[...] = jnp.zeros(h_ref.shape, F32)

    group = lambda v, r: v[r * SUBLANES:(r + 1) * SUBLANES, :]
    sub = lax.broadcasted_iota(jnp.int32, (SUBLANES, D_MODEL), 0)

    def project(s):
        h = _rms(x_ref[s], g_ref[...], NORM_EPS).astype(BF16)
        h = jnp.dot(perm_ref[...], h, preferred_element_type=F32).astype(BF16)
        gate = _gelu_tanh(jnp.dot(h, wy_ref[...], preferred_element_type=F32))
        xp = jnp.dot(h, wx_ref[...], preferred_element_type=F32)
        return gate, xp

    def conv_and_gates(s, xp):
        lead = []
        for k in range(CONV_WIDTH - 1):
            cur = group(xp, REC_STEPS - (CONV_WIDTH - 1) + k)
            prev_last = tail_ref[s, (k + 1) * SUBLANES - 1:(k + 1) * SUBLANES, :]
            lead.append(jnp.where(sub == 0, prev_last, pltpu.roll(cur, 1, 0)))
        tail_ref[s] = xp[TM_REC - n_tail:, :]
        ext = jnp.concatenate(lead + [xp], axis=0)
        cw = cw_ref[...]
        xb = cb_ref[...]
        for j in range(CONV_WIDTH):
            xb = xb + ext[j * SUBLANES:j * SUBLANES + TM_REC, :] * cw[j:j + 1, :]
        xb16 = xb.astype(BF16)

        def block_diag(w_ref):
            return jnp.concatenate(
                [jnp.dot(xb16[:, n * RG_BLOCK:(n + 1) * RG_BLOCK], w_ref[n],
                         preferred_element_type=F32) for n in range(RG_HEADS)], axis=1)

        return xb, block_diag(wa_ref), block_diag(wi_ref)

    def recur(s, gate, xb, ra, ri):
        r = jax.nn.sigmoid(ra + ba_ref[...])
        i = jax.nn.sigmoid(ri + bi_ref[...])
        neg_lam = -lam_ref[...]
        softplus = jnp.maximum(neg_lam, 0.0) + jnp.log1p(jnp.exp(-jnp.abs(neg_lam)))
        rate = (-RG_C * math.log2(math.e)) * softplus
        a = jnp.exp2(r * rate)
        y = 1.0 - a * a
        u = (y * lax.rsqrt(jnp.maximum(y, jnp.finfo(F32).tiny))) * (i * xb)
        hz, pz = [group(u, 0)], [group(a, 0)]
        for k in range(1, REC_STEPS):
            a_k = group(a, k)
            hz.append(a_k * hz[-1] + group(u, k))
            pz.append(a_k * pz[-1])
        start = h_ref[s]
        starts = []
        for b in range(SUBLANES):
            starts.append(start)
            start = hz[-1][b:b + 1, :] + pz[-1][b:b + 1, :] * start
        h_ref[s] = start
        starts = jnp.concatenate(starts, axis=0)
        y = jnp.concatenate([(hz[k] + pz[k] * starts) * group(gate, k) for k in range(REC_STEPS)],
                            axis=0).astype(BF16)
        out_ref[s] = jnp.dot(unperm_ref[...], y, preferred_element_type=F32).astype(BF16)

    streams = range(REC_STREAMS)
    proj = [project(s) for s in streams]
    mid = [conv_and_gates(s, proj[s][1]) for s in streams]
    for s in streams:
        recur(s, proj[s][0], *mid[s])


def _rglru(x, g, wx, wy, cw, cb, wa, ba, wi, bi, lam):
    batch, seq, _ = x.shape
    src = (jnp.arange(TM_REC) % SUBLANES) * REC_STEPS + jnp.arange(TM_REC) // SUBLANES
    perm = (src[:, None] == jnp.arange(TM_REC)[None, :]).astype(BF16)
    c2 = lambda b, t: (0, 0)
    c3 = lambda b, t: (0, 0, 0)
    return pl.pallas_call(
        _rglru_kernel,
        grid=(batch // REC_STREAMS, seq // TM_REC),
        in_specs=[
            pl.BlockSpec((REC_STREAMS, TM_REC, D_MODEL), lambda b, t: (b, t, 0)),
            pl.BlockSpec((1, D_MODEL), c2),
            pl.BlockSpec((TM_REC, TM_REC), c2),
            pl.BlockSpec((TM_REC, TM_REC), c2),
            pl.BlockSpec((D_MODEL, D_MODEL), c2),
            pl.BlockSpec((D_MODEL, D_MODEL), c2),
            pl.BlockSpec((CONV_WIDTH, D_MODEL), c2),
            pl.BlockSpec((1, D_MODEL), c2),
            pl.BlockSpec((RG_HEADS, RG_BLOCK, RG_BLOCK), c3),
            pl.BlockSpec((1, D_MODEL), c2),
            pl.BlockSpec((RG_HEADS, RG_BLOCK, RG_BLOCK), c3),
            pl.BlockSpec((1, D_MODEL), c2),
            pl.BlockSpec((1, D_MODEL), c2),
        ],
        out_specs=pl.BlockSpec((REC_STREAMS, TM_REC, D_MODEL), lambda b, t: (b, t, 0)),
        out_shape=jax.ShapeDtypeStruct((batch, seq, D_MODEL), BF16),
        scratch_shapes=[
            pltpu.VMEM((REC_STREAMS, (CONV_WIDTH - 1) * SUBLANES, D_MODEL), F32),
            pltpu.VMEM((REC_STREAMS, 1, D_MODEL), F32),
        ],
        compiler_params=pltpu.CompilerParams(
            dimension_semantics=("arbitrary", "arbitrary"), vmem_limit_bytes=VMEM_LIMIT),
        name="rglru",
    )(x, g, perm, perm.T, wx, wy, cw, cb, wa, ba, wi, bi, lam)


def _rope_tables(seq):
    half = HEAD_DIM // 2
    inv_freq = 1.0 / (ROPE_THETA ** (jnp.arange(0, HEAD_DIM, 2, dtype=F32) / HEAD_DIM))
    ang = jnp.arange(seq, dtype=jnp.int32).astype(F32)[:, None] * inv_freq[None, :]
    cos = jnp.tile(jnp.cos(ang), (1, LANES // half))
    sin = jnp.tile(jnp.sin(ang), (1, LANES // half))
    sign = jnp.where(jnp.arange(LANES) < LANES // 2, -1.0, 1.0).astype(F32)
    return cos, sin * sign[None, :]


def _permute_heads(w):
    d_in = w.shape[0]
    half = HEAD_DIM // 2
    return w.reshape(d_in, ATTN_HEADS, 2, 2, half).transpose(0, 1, 3, 2, 4).reshape(d_in, -1)


def kernel(x, mix_norm_g, mlp_norm_g, attn_w_qkv, attn_w_o, attn_lq1, attn_lk1, attn_lq2, attn_lk2,
           attn_subln_g, rec_w_x, rec_w_y, rec_conv_w, rec_conv_b, rec_w_a, rec_b_a, rec_w_i, rec_b_i,
           rec_lambda, rec_w_o, mlp_w1, mlp_w2, final_norm_g):
    batch, seq, d = x.shape
    x2d = x.reshape(batch * seq, d)
    row = lambda v: v.reshape(1, -1)

    lambda_init = 0.8 - 0.6 * math.exp(-0.3 * 0)
    w_qkv = attn_w_qkv[0]
    wqt = _permute_heads(w_qkv[:, :d]).T.astype(BF16)
    wk = _permute_heads(w_qkv[:, d:2 * d]).astype(BF16)
    wvt = w_qkv[:, 2 * d:].T.astype(BF16)
    cos, sin = _rope_tables(seq)
    qt, k, vt = _qkv_rope(x2d, row(mix_norm_g[0]), wqt, wk, wvt, cos, sin, batch, seq)
    lqk = jnp.stack([attn_lq1[0], attn_lk1[0], attn_lq2[0], attn_lk2[0]])
    o = _diff_attn(lqk, attn_subln_g[0].reshape(V_DIM, 1),
                   qt, k.reshape(batch, seq, d), vt, lambda_init)
    x2d = _proj_mlp(x2d, o.reshape(batch * seq, d), attn_w_o[0].astype(BF16), row(mlp_norm_g[0]),
                    mlp_w1[0].astype(BF16), mlp_w2[0].astype(BF16), row(final_norm_g), False)

    gated = _rglru(x2d.reshape(batch, seq, d), row(mix_norm_g[1]),
                   rec_w_x[0].astype(BF16), rec_w_y[0].astype(BF16), rec_conv_w[0], row(rec_conv_b[0]),
                   rec_w_a[0].astype(BF16), row(rec_b_a[0]), rec_w_i[0].astype(BF16), row(rec_b_i[0]),
                   row(rec_lambda[0]))
    out = _proj_mlp(x2d, gated.reshape(batch * seq, d), rec_w_o[0].astype(BF16), row(mlp_norm_g[1]),
                    mlp_w1[1].astype(BF16), mlp_w2[1].astype(BF16), row(final_norm_g), True)
    return out.reshape(batch, seq, d)
```

```python
import functools
import math

import jax
import jax.numpy as jnp
from jax import lax
from jax.experimental import pallas as pl
from jax.experimental.pallas import tpu as pltpu

F32 = jnp.float32
BF16 = jnp.bfloat16

D_MODEL = 1024
CHUNK = 64
ATTN_HEADS = 8
HEAD_DIM = 64
V_DIM = 128
VT_ROWS = V_DIM + 16
ROPE_THETA = 10000.0
RG_BLOCK = 256
RG_HEADS = 4
CONV_WIDTH = 4
RG_C = 8.0
D_FF = 4 * D_MODEL
NORM_EPS = 1e-6
SUBLN_EPS = 1e-5

LANES = 128
SUBLANES = 8
VMEM_LIMIT = 56 * 1024 * 1024

TM_QKV = 1024
TQ = 512
TK = TQ
N_STREAMS = 4
BF16_ROWS = 2 * SUBLANES
assert LANES == 2 * CHUNK and TK == TQ
TM_MLP = 512
FF_CHUNK = 2048
TM_REC = 256
REC_STEPS = TM_REC // SUBLANES
REC_STREAMS = 4

NT = (((1,), (1,)), ((), ()))
Q_SCALE = (HEAD_DIM ** -0.5) * math.log2(math.e)


def _rms(x, g, eps):
    ms = jnp.mean(x * x, axis=-1, keepdims=True)
    return (x * lax.rsqrt(ms + eps)) * g


def _gelu_tanh(x):
    c0 = math.sqrt(2.0 / math.pi)
    inner = x * (c0 + (c0 * 0.044715) * (x * x))
    return (0.5 * x) * (1.0 + jnp.tanh(inner))


def _qkv_kernel(x_ref, g_ref, wqt_ref, wk_ref, wvt_ref, cos_ref, sin_ref, cos_t_ref, sin_t_ref,
                qt_ref, k_ref, vt_ref):
    y = _rms(x_ref[...], g_ref[...], NORM_EPS).astype(BF16)
    half = LANES // 2
    qt = lax.dot_general(wqt_ref[...], y, NT, preferred_element_type=F32)
    cos_t = cos_t_ref[...]
    sin_t = sin_t_ref[...]
    for h in range(ATTN_HEADS):
        t = qt[h * LANES:(h + 1) * LANES, :]
        partner = jnp.concatenate([t[half:, :], t[:half, :]], axis=0)
        qt_ref[0, h] = ((t * cos_t + partner * sin_t) * Q_SCALE).astype(BF16)
    cos = cos_ref[...]
    sin = sin_ref[...]
    k = jnp.dot(y, wk_ref[...], preferred_element_type=F32)
    for h in range(ATTN_HEADS):
        sl = slice(h * LANES, (h + 1) * LANES)
        t = k[:, sl]
        k_ref[:, sl] = (t * cos + pltpu.roll(t, half, 1) * sin).astype(BF16)
    vt = lax.dot_general(wvt_ref[...], y, NT, preferred_element_type=F32)
    for h in range(ATTN_HEADS):
        for t in range(TM_QKV // TK):
            vt_ref[0, h, t, :V_DIM, :] = vt[h * V_DIM:(h + 1) * V_DIM, t * TK:(t + 1) * TK].astype(BF16)
            vt_ref[0, h, t, V_DIM:, :] = jnp.ones((VT_ROWS - V_DIM, TK), BF16)


def _qkv_rope(x2d, g, wqt, wk, wvt, cos, sin, batch, seq):
    n_tok = x2d.shape[0]
    tiles_per_seq = seq // TM_QKV
    const = lambda i: (0, 0)
    return pl.pallas_call(
        _qkv_kernel,
        grid=(n_tok // TM_QKV,),
        in_specs=[
            pl.BlockSpec((TM_QKV, D_MODEL), lambda i: (i, 0)),
            pl.BlockSpec((1, D_MODEL), const),
            pl.BlockSpec((D_MODEL, D_MODEL), const),
            pl.BlockSpec((D_MODEL, D_MODEL), const),
            pl.BlockSpec((D_MODEL, D_MODEL), const),
            pl.BlockSpec((TM_QKV, LANES), lambda i: (i % tiles_per_seq, 0)),
            pl.BlockSpec((TM_QKV, LANES), lambda i: (i % tiles_per_seq, 0)),
            pl.BlockSpec((LANES, TM_QKV), lambda i: (0, i % tiles_per_seq)),
            pl.BlockSpec((LANES, TM_QKV), lambda i: (0, i % tiles_per_seq)),
        ],
        out_specs=[
            pl.BlockSpec((1, ATTN_HEADS, LANES, TM_QKV),
                         lambda i: (i // tiles_per_seq, 0, 0, i % tiles_per_seq)),
            pl.BlockSpec((TM_QKV, D_MODEL), lambda i: (i, 0)),
            pl.BlockSpec((1, ATTN_HEADS, TM_QKV // TK, VT_ROWS, TK),
                         lambda i: (i // tiles_per_seq, 0, i % tiles_per_seq, 0, 0)),
        ],
        out_shape=[
            jax.ShapeDtypeStruct((batch, ATTN_HEADS, LANES, seq), BF16),
            jax.ShapeDtypeStruct((n_tok, D_MODEL), BF16),
            jax.ShapeDtypeStruct((batch, ATTN_HEADS, seq // TK, VT_ROWS, TK), BF16),
        ],
        compiler_params=pltpu.CompilerParams(
            dimension_semantics=("arbitrary",), vmem_limit_bytes=VMEM_LIMIT),
        name="qkv_rope",
    )(x2d, g, wqt, wk, wvt, cos, sin, cos.T, sin.T)


def _attn_kernel(lqk_ref, g_ref, q_ref, k_ref, vt_ref, o_ref, m_ref, l_ref, acc_ref, alpha_ref, p_ref,
                 *, lambda_init):
    qi = pl.program_id(2)

    feature = lax.broadcasted_iota(jnp.int32, (LANES, TQ), 0)
    is_map1 = (feature // (HEAD_DIM // 2)) % 2 == 0
    keep1 = jnp.where(is_map1, 1.0, 0.0).astype(BF16)
    keep2 = jnp.where(is_map1, 0.0, 1.0).astype(BF16)
    q_st = []
    for s in range(N_STREAMS):
        qt = q_ref[0, s]
        q_st.append(jnp.concatenate([qt * keep1, qt * keep2], axis=1))

    def score_matmul(s, j, own):
        half = TK // 2

        def keys(lo, n):
            rows = pl.ds(pl.multiple_of(j * TK + lo, half), n)
            return k_ref[0, rows, s * LANES:(s + 1) * LANES]

        if not own:
            full = jnp.dot(keys(0, TK), q_st[s], preferred_element_type=F32)
            return lambda g, cols: full[g:g + BF16_ROWS, cols]
        top = jnp.dot(keys(0, half), q_st[s], preferred_element_type=F32)
        late = jnp.concatenate([q_st[s][:, m * TQ + half:(m + 1) * TQ] for m in range(2)], axis=1)
        bottom = jnp.dot(keys(half, half), late, preferred_element_type=F32)

        def read(g, cols):
            if g < half:
                return top[g:g + BF16_ROWS, cols]
            m, q0 = divmod(cols.start, TQ)
            c0 = m * half + q0 - half
            return bottom[g - half:g - half + BF16_ROWS, c0:c0 + cols.stop - cols.start]

        return read

    def value_matmul(s, j):
        return jnp.dot(vt_ref[0, s, j], p_ref[s], preferred_element_type=F32)

    def accumulate(s, pv):
        acc_ref[s] = alpha_ref[s] * acc_ref[s] + pv[:V_DIM, :]
        l_ref[s] = alpha_ref[s] * l_ref[s] + pv[V_DIM:V_DIM + 1, :]

    def matmuls(j, own, pending):
        scores, pv = [], {}
        for s in range(N_STREAMS):
            scores.append(score_matmul(s, j, own))
            if pending is not None and s >= 1:
                pv[s - 1] = value_matmul(s - 1, pending)
        if pending is not None:
            pv[N_STREAMS - 1] = value_matmul(N_STREAMS - 1, pending)
        return scores, pv

    upper_half = lax.broadcasted_iota(jnp.int32, (BF16_ROWS, LANES), 1) >= CHUNK

    def key_tile(j, own, pending):
        scores, pv = matmuls(j, own, pending)
        width = LANES if own else 2 * TQ
        lane_blocks = [slice(c, c + width) for c in range(0, 2 * TQ, width)]
        cache = {}

        def block(s, g, cols):
            if (s, g, cols.start) not in cache:
                key_chunk = g // CHUNK
                qry_chunk = (cols.start % TQ) // CHUNK
                if own and qry_chunk + 1 < key_chunk:
                    blk = None
                else:
                    blk = scores[s](g, cols)
                    if own and qry_chunk < key_chunk:
                        blk = jnp.where(upper_half, blk, -jnp.inf)
                cache[s, g, cols.start] = blk
            return cache[s, g, cols.start]

        for s in range(N_STREAMS):
            if pending is not None:
                accumulate(s, pv[s])
            mx = {}
            for g in range(0, TK, BF16_ROWS):
                for cols in lane_blocks:
                    blk = block(s, g, cols)
                    if blk is not None:
                        seen = mx.get(cols.start)
                        mx[cols.start] = blk if seen is None else jnp.maximum(seen, blk)
            m_old = m_ref[s]
            col_max = jnp.concatenate(
                [jnp.max(mx[cols.start], axis=0, keepdims=True) for cols in lane_blocks], axis=1)
            m_new = jnp.maximum(m_old, col_max)
            alpha_ref[s] = jnp.exp2(m_old - m_new)
            m_ref[s] = m_new
            for g in range(0, TK, BF16_ROWS):
                for cols in lane_blocks:
                    blk = block(s, g, cols)
                    if blk is None:
                        p = jnp.zeros((BF16_ROWS, width), BF16)
                    else:
                        p = jnp.exp2(blk - m_new[:, cols]).astype(BF16)
                    p_ref[s, g:g + BF16_ROWS, cols] = p

    m_ref[...] = jnp.full(m_ref.shape, -1e30, F32)
    l_ref[...] = jnp.zeros(l_ref.shape, F32)
    acc_ref[...] = jnp.zeros(acc_ref.shape, F32)
    key_tile(qi, True, None)

    def body(j, pending):
        key_tile(j, False, pending)
        return j

    pending = lax.fori_loop(0, qi, body, qi)
    for s in range(N_STREAMS):
        accumulate(s, value_matmul(s, pending))

    lqk = lqk_ref[...]
    lam = (jnp.exp(jnp.sum(lqk[0:1] * lqk[1:2], axis=1, keepdims=True))
           - jnp.exp(jnp.sum(lqk[2:3] * lqk[3:4], axis=1, keepdims=True))
           + lambda_init)
    for s in range(N_STREAMS):
        acc = acc_ref[s]
        l = l_ref[s]
        o = acc[:, :TQ] / l[:, :TQ] - lam * (acc[:, TQ:] / l[:, TQ:])
        ms = jnp.mean(o * o, axis=0, keepdims=True)
        y = ((o * lax.rsqrt(ms + SUBLN_EPS)) * g_ref[...]) * (1.0 - lambda_init)
        o_ref[0, :, s * LANES:(s + 1) * LANES] = y.T.astype(BF16)


def _diff_attn(lqk, g_col, qt, k, vt, lambda_init):
    batch, seq, _ = k.shape
    width = N_STREAMS * LANES
    kernel = functools.partial(_attn_kernel, lambda_init=lambda_init)
    return pl.pallas_call(
        kernel,
        grid=(batch, ATTN_HEADS // N_STREAMS, seq // TQ),
        in_specs=[
            pl.BlockSpec((4, HEAD_DIM), lambda b, h, i: (0, 0)),
            pl.BlockSpec((V_DIM, 1), lambda b, h, i: (0, 0)),
            pl.BlockSpec((1, N_STREAMS, LANES, TQ), lambda b, h, i: (b, h, 0, i)),
            pl.BlockSpec((1, seq, width), lambda b, h, i: (b, 0, h)),
            pl.BlockSpec((1, N_STREAMS, seq // TK, VT_ROWS, TK), lambda b, h, i: (b, h, 0, 0, 0)),
        ],
        out_specs=pl.BlockSpec((1, TQ, width), lambda b, h, i: (b, i, h)),
        out_shape=jax.ShapeDtypeStruct((batch, seq, ATTN_HEADS * V_DIM), BF16),
        scratch_shapes=[
            pltpu.VMEM((N_STREAMS, 1, 2 * TQ), F32),
            pltpu.VMEM((N_STREAMS, 1, 2 * TQ), F32),
            pltpu.VMEM((N_STREAMS, V_DIM, 2 * TQ), F32),
            pltpu.VMEM((N_STREAMS, 1, 2 * TQ), F32),
            pltpu.VMEM((N_STREAMS, TK, 2 * TQ), BF16),
        ],
        compiler_params=pltpu.CompilerParams(
            dimension_semantics=("arbitrary", "arbitrary", "arbitrary"),
            vmem_limit_bytes=VMEM_LIMIT),
        name="diff_attn",
    )(lqk, g_col, qt, k, vt)


def _proj_mlp_kernel(x_ref, a_ref, wo_ref, g_ref, w1_ref, w2_ref, gf_ref, out_ref, *, final_norm):
    x1 = x_ref[...] + jnp.dot(a_ref[...], wo_ref[...], preferred_element_type=F32)
    h = _rms(x1, g_ref[...], NORM_EPS).astype(BF16)
    acc = x1
    for c in range(D_FF // FF_CHUNK):
        sl = slice(c * FF_CHUNK, (c + 1) * FF_CHUNK)
        u = jnp.dot(h, w1_ref[:, sl], preferred_element_type=F32)
        u = jnp.square(jnp.maximum(u, 0.0)).astype(BF16)
        acc = acc + jnp.dot(u, w2_ref[sl, :], preferred_element_type=F32)
    if final_norm:
        acc = _rms(acc, gf_ref[...], NORM_EPS)
    out_ref[...] = acc


def _proj_mlp(x2d, a2d, wo, g, w1, w2, gf, final_norm):
    n_tok = x2d.shape[0]
    const = lambda i: (0, 0)
    resident = pl.Buffered(1)
    kernel = functools.partial(_proj_mlp_kernel, final_norm=final_norm)
    return pl.pallas_call(
        kernel,
        grid=(n_tok // TM_MLP,),
        in_specs=[
            pl.BlockSpec((TM_MLP, D_MODEL), lambda i: (i, 0)),
            pl.BlockSpec((TM_MLP, D_MODEL), lambda i: (i, 0)),
            pl.BlockSpec((D_MODEL, D_MODEL), const, pipeline_mode=resident),
            pl.BlockSpec((1, D_MODEL), const),
            pl.BlockSpec((D_MODEL, D_FF), const, pipeline_mode=resident),
            pl.BlockSpec((D_FF, D_MODEL), const, pipeline_mode=resident),
            pl.BlockSpec((1, D_MODEL), const),
        ],
        out_specs=pl.BlockSpec((TM_MLP, D_MODEL), lambda i: (i, 0)),
        out_shape=jax.ShapeDtypeStruct((n_tok, D_MODEL), F32),
        compiler_params=pltpu.CompilerParams(
            dimension_semantics=("arbitrary",), vmem_limit_bytes=VMEM_LIMIT),
        name="proj_mlp_final" if final_norm else "proj_mlp",
    )(x2d, a2d, wo, g, w1, w2, gf)


def _rglru_kernel(x_ref, g_ref, perm_ref, unperm_ref, wx_ref, wy_ref, cw_ref, cb_ref, wa_ref, ba_ref,
                  wi_ref, bi_ref, lam_ref, out_ref, tail_ref, h_ref):
    t = pl.program_id(1)
    n_tail = (CONV_WIDTH - 1) * SUBLANES

    @pl.when(t == 0)
    def _():
        tail_ref[...] = jnp.zeros(tail_ref.shape, F32)
        h_ref[...] = jnp.zeros(h_ref.shape, F32)

    group = lambda v, r: v[r * SUBLANES:(r + 1) * SUBLANES, :]
    sub = lax.broadcasted_iota(jnp.int32, (SUBLANES, D_MODEL), 0)

    def project(s):
        h = _rms(x_ref[s], g_ref[...], NORM_EPS).astype(BF16)
        h = jnp.dot(perm_ref[...], h, preferred_element_type=F32).astype(BF16)
        gate = _gelu_tanh(jnp.dot(h, wy_ref[...], preferred_element_type=F32))
        xp = jnp.dot(h, wx_ref[...], preferred_element_type=F32)
        return gate, xp

    def conv_and_gates(s, xp):
        lead = []
        for k in range(CONV_WIDTH - 1):
            cur = group(xp, REC_STEPS - (CONV_WIDTH - 1) + k)
            prev_last = tail_ref[s, (k + 1) * SUBLANES - 1:(k + 1) * SUBLANES, :]
            lead.append(jnp.where(sub == 0, prev_last, pltpu.roll(cur, 1, 0)))
        tail_ref[s] = xp[TM_REC - n_tail:, :]
        ext = jnp.concatenate(lead + [xp], axis=0)
        cw = cw_ref[...]
        xb = cb_ref[...]
        for j in range(CONV_WIDTH):
            xb = xb + ext[j * SUBLANES:j * SUBLANES + TM_REC, :] * cw[j:j + 1, :]
        xb16 = xb.astype(BF16)

        def block_diag(w_ref):
            return jnp.concatenate(
                [jnp.dot(xb16[:, n * RG_BLOCK:(n + 1) * RG_BLOCK], w_ref[n],
                         preferred_element_type=F32) for n in range(RG_HEADS)], axis=1)

        return xb, block_diag(wa_ref), block_diag(wi_ref)

    def recur(s, gate, xb, ra, ri):
        r = jax.nn.sigmoid(ra + ba_ref[...])
        i = jax.nn.sigmoid(ri + bi_ref[...])
        neg_lam = -lam_ref[...]
        softplus = jnp.maximum(neg_lam, 0.0) + jnp.log1p(jnp.exp(-jnp.abs(neg_lam)))
        rate = (-RG_C * math.log2(math.e)) * softplus
        a = jnp.exp2(r * rate)
        y = 1.0 - a * a
        u = (y * lax.rsqrt(jnp.maximum(y, jnp.finfo(F32).tiny))) * (i * xb)
        hz, pz = [group(u, 0)], [group(a, 0)]
        for k in range(1, REC_STEPS):
            a_k = group(a, k)
            hz.append(a_k * hz[-1] + group(u, k))
            pz.append(a_k * pz[-1])
        start = h_ref[s]
        starts = []
        for b in range(SUBLANES):
            starts.append(start)
            start = hz[-1][b:b + 1, :] + pz[-1][b:b + 1, :] * start
        h_ref[s] = start
        starts = jnp.concatenate(starts, axis=0)
        y = jnp.concatenate([(hz[k] + pz[k] * starts) * group(gate, k) for k in range(REC_STEPS)],
                            axis=0).astype(BF16)
        out_ref[s] = jnp.dot(unperm_ref[...], y, preferred_element_type=F32).astype(BF16)

    streams = range(REC_STREAMS)
    proj = [project(s) for s in streams]
    mid = [conv_and_gates(s, proj[s][1]) for s in streams]
    for s in streams:
        recur(s, proj[s][0], *mid[s])


def _rglru(x, g, wx, wy, cw, cb, wa, ba, wi, bi, lam):
    batch, seq, _ = x.shape
    src = (jnp.arange(TM_REC) % SUBLANES) * REC_STEPS + jnp.arange(TM_REC) // SUBLANES
    perm = (src[:, None] == jnp.arange(TM_REC)[None, :]).astype(BF16)
    c2 = lambda b, t: (0, 0)
    c3 = lambda b, t: (0, 0, 0)
    return pl.pallas_call(
        _rglru_kernel,
        grid=(batch // REC_STREAMS, seq // TM_REC),
        in_specs=[
            pl.BlockSpec((REC_STREAMS, TM_REC, D_MODEL), lambda b, t: (b, t, 0)),
            pl.BlockSpec((1, D_MODEL), c2),
            pl.BlockSpec((TM_REC, TM_REC), c2),
            pl.BlockSpec((TM_REC, TM_REC), c2),
            pl.BlockSpec((D_MODEL, D_MODEL), c2),
            pl.BlockSpec((D_MODEL, D_MODEL), c2),
            pl.BlockSpec((CONV_WIDTH, D_MODEL), c2),
            pl.BlockSpec((1, D_MODEL), c2),
            pl.BlockSpec((RG_HEADS, RG_BLOCK, RG_BLOCK), c3),
            pl.BlockSpec((1, D_MODEL), c2),
            pl.BlockSpec((RG_HEADS, RG_BLOCK, RG_BLOCK), c3),
            pl.BlockSpec((1, D_MODEL), c2),
            pl.BlockSpec((1, D_MODEL), c2),
        ],
        out_specs=pl.BlockSpec((REC_STREAMS, TM_REC, D_MODEL), lambda b, t: (b, t, 0)),
        out_shape=jax.ShapeDtypeStruct((batch, seq, D_MODEL), BF16),
        scratch_shapes=[
            pltpu.VMEM((REC_STREAMS, (CONV_WIDTH - 1) * SUBLANES, D_MODEL), F32),
            pltpu.VMEM((REC_STREAMS, 1, D_MODEL), F32),
        ],
        compiler_params=pltpu.CompilerParams(
            dimension_semantics=("arbitrary", "arbitrary"), vmem_limit_bytes=VMEM_LIMIT),
        name="rglru",
    )(x, g, perm, perm.T, wx, wy, cw, cb, wa, ba, wi, bi, lam)


def _rope_tables(seq):
    half = HEAD_DIM // 2
    inv_freq = 1.0 / (ROPE_THETA ** (jnp.arange(0, HEAD_DIM, 2, dtype=F32) / HEAD_DIM))
    ang = jnp.arange(seq, dtype=jnp.int32).astype(F32)[:, None] * inv_freq[None, :]
    cos = jnp.tile(jnp.cos(ang), (1, LANES // half))
    sin = jnp.tile(jnp.sin(ang), (1, LANES // half))
    sign = jnp.where(jnp.arange(LANES) < LANES // 2, -1.0, 1.0).astype(F32)
    return cos, sin * sign[None, :]


def _permute_heads(w):
    d_in = w.shape[0]
    half = HEAD_DIM // 2
    return w.reshape(d_in, ATTN_HEADS, 2, 2, half).transpose(0, 1, 3, 2, 4).reshape(d_in, -1)


def kernel(x, mix_norm_g, mlp_norm_g, attn_w_qkv, attn_w_o, attn_lq1, attn_lk1, attn_lq2, attn_lk2,
           attn_subln_g, rec_w_x, rec_w_y, rec_conv_w, rec_conv_b, rec_w_a, rec_b_a, rec_w_i, rec_b_i,
           rec_lambda, rec_w_o, mlp_w1, mlp_w2, final_norm_g):
    batch, seq, d = x.shape
    x2d = x.reshape(batch * seq, d)
    row = lambda v: v.reshape(1, -1)

    lambda_init = 0.8 - 0.6 * math.exp(-0.3 * 0)
    w_qkv = attn_w_qkv[0]
    wqt = _permute_heads(w_qkv[:, :d]).T.astype(BF16)
    wk = _permute_heads(w_qkv[:, d:2 * d]).astype(BF16)
    wvt = w_qkv[:, 2 * d:].T.astype(BF16)
    cos, sin = _rope_tables(seq)
    qt, k, vt = _qkv_rope(x2d, row(mix_norm_g[0]), wqt, wk, wvt, cos, sin, batch, seq)
    lqk = jnp.stack([attn_lq1[0], attn_lk1[0], attn_lq2[0], attn_lk2[0]])
    o = _diff_attn(lqk, attn_subln_g[0].reshape(V_DIM, 1),
                   qt, k.reshape(batch, seq, d), vt, lambda_init)
    x2d = _proj_mlp(x2d, o.reshape(batch * seq, d), attn_w_o[0].astype(BF16), row(mlp_norm_g[0]),
                    mlp_w1[0].astype(BF16), mlp_w2[0].astype(BF16), row(final_norm_g), False)

    gated = _rglru(x2d.reshape(batch, seq, d), row(mix_norm_g[1]),
                   rec_w_x[0].astype(BF16), rec_w_y[0].astype(BF16), rec_conv_w[0], row(rec_conv_b[0]),
                   rec_w_a[0].astype(BF16), row(rec_b_a[0]), rec_w_i[0].astype(BF16), row(rec_b_i[0]),
                   row(rec_lambda[0]))
    out = _proj_mlp(x2d, gated.reshape(batch * seq, d), rec_w_o[0].astype(BF16), row(mlp_norm_g[1]),
                    mlp_w1[1].astype(BF16), mlp_w2[1].astype(BF16), row(final_norm_g), True)
    return out.reshape(batch, seq, d)
```

```python
import functools
import math

import jax
import jax.numpy as jnp
from jax import lax
from jax.experimental import pallas as pl
from jax.experimental.pallas import tpu as pltpu

F32 = jnp.float32
BF16 = jnp.bfloat16

D_MODEL = 1024
CHUNK = 64
ATTN_HEADS = 8
HEAD_DIM = 64
V_DIM = 128
ROPE_THETA = 10000.0
RG_BLOCK = 256
RG_HEADS = 4
CONV_WIDTH = 4
RG_C = 8.0
D_FF = 4 * D_MODEL
NORM_EPS = 1e-6
SUBLN_EPS = 1e-5

LANES = 128
SUBLANES = 8
VMEM_LIMIT = 56 * 1024 * 1024

TM_QKV = 1024
TQ = 512
TK = TQ
N_STREAMS = 4
BF16_ROWS = 2 * SUBLANES
VT_ROWS = V_DIM + BF16_ROWS
assert LANES == 2 * CHUNK and TK == TQ
TM_MLP = 512
FF_CHUNK = 2048
TM_REC = 256
REC_STEPS = TM_REC // SUBLANES
REC_STREAMS = 4

NT = (((1,), (1,)), ((), ()))
Q_SCALE = (HEAD_DIM ** -0.5) * math.log2(math.e)


def _rms(x, g, eps):
    ms = jnp.mean(x * x, axis=-1, keepdims=True)
    return (x * lax.rsqrt(ms + eps)) * g


def _gelu_tanh(x):
    c0 = math.sqrt(2.0 / math.pi)
    inner = x * (c0 + (c0 * 0.044715) * (x * x))
    return (0.5 * x) * (1.0 + jnp.tanh(inner))


def _qkv_kernel(x_ref, g_ref, wqt_ref, wk_ref, wvt_ref, cos_ref, sin_ref, cos_t_ref, sin_t_ref,
                qt_ref, k_ref, vt_ref):
    y = _rms(x_ref[...], g_ref[...], NORM_EPS).astype(BF16)
    half = LANES // 2
    qt = lax.dot_general(wqt_ref[...], y, NT, preferred_element_type=F32)
    cos_t = cos_t_ref[...]
    sin_t = sin_t_ref[...]
    for h in range(ATTN_HEADS):
        t = qt[h * LANES:(h + 1) * LANES, :]
        partner = jnp.concatenate([t[half:, :], t[:half, :]], axis=0)
        qt_ref[0, h] = ((t * cos_t + partner * sin_t) * Q_SCALE).astype(BF16)
    cos = cos_ref[...]
    sin = sin_ref[...]
    k = jnp.dot(y, wk_ref[...], preferred_element_type=F32)
    for h in range(ATTN_HEADS):
        sl = slice(h * LANES, (h + 1) * LANES)
        t = k[:, sl]
        k_ref[:, sl] = (t * cos + pltpu.roll(t, half, 1) * sin).astype(BF16)
    vt = lax.dot_general(wvt_ref[...], y, NT, preferred_element_type=F32)
    for h in range(ATTN_HEADS):
        for t in range(TM_QKV // TK):
            vt_ref[0, h, t, :V_DIM, :] = vt[h * V_DIM:(h + 1) * V_DIM, t * TK:(t + 1) * TK].astype(BF16)
            vt_ref[0, h, t, V_DIM:, :] = jnp.ones((VT_ROWS - V_DIM, TK), BF16)


def _qkv_rope(x2d, g, wqt, wk, wvt, cos, sin, batch, seq):
    n_tok = x2d.shape[0]
    tiles_per_seq = seq // TM_QKV
    const = lambda i: (0, 0)
    return pl.pallas_call(
        _qkv_kernel,
        grid=(n_tok // TM_QKV,),
        in_specs=[
            pl.BlockSpec((TM_QKV, D_MODEL), lambda i: (i, 0)),
            pl.BlockSpec((1, D_MODEL), const),
            pl.BlockSpec((D_MODEL, D_MODEL), const),
            pl.BlockSpec((D_MODEL, D_MODEL), const),
            pl.BlockSpec((D_MODEL, D_MODEL), const),
            pl.BlockSpec((TM_QKV, LANES), lambda i: (i % tiles_per_seq, 0)),
            pl.BlockSpec((TM_QKV, LANES), lambda i: (i % tiles_per_seq, 0)),
            pl.BlockSpec((LANES, TM_QKV), lambda i: (0, i % tiles_per_seq)),
            pl.BlockSpec((LANES, TM_QKV), lambda i: (0, i % tiles_per_seq)),
        ],
        out_specs=[
            pl.BlockSpec((1, ATTN_HEADS, LANES, TM_QKV),
                         lambda i: (i // tiles_per_seq, 0, 0, i % tiles_per_seq)),
            pl.BlockSpec((TM_QKV, D_MODEL), lambda i: (i, 0)),
            pl.BlockSpec((1, ATTN_HEADS, TM_QKV // TK, VT_ROWS, TK),
                         lambda i: (i // tiles_per_seq, 0, i % tiles_per_seq, 0, 0)),
        ],
        out_shape=[
            jax.ShapeDtypeStruct((batch, ATTN_HEADS, LANES, seq), BF16),
            jax.ShapeDtypeStruct((n_tok, D_MODEL), BF16),
            jax.ShapeDtypeStruct((batch, ATTN_HEADS, seq // TK, VT_ROWS, TK), BF16),
        ],
        compiler_params=pltpu.CompilerParams(
            dimension_semantics=("arbitrary",), vmem_limit_bytes=VMEM_LIMIT),
        name="qkv_rope",
    )(x2d, g, wqt, wk, wvt, cos, sin, cos.T, sin.T)


def _attn_kernel(lqk_ref, g_ref, q_ref, k_ref, vt_ref, o_ref, m_ref, l_ref, acc_ref, alpha_ref, p_ref,
                 *, lambda_init):
    qi = pl.program_id(2)

    feature = lax.broadcasted_iota(jnp.int32, (LANES, TQ), 0)
    is_map1 = (feature // (HEAD_DIM // 2)) % 2 == 0
    keep1 = jnp.where(is_map1, 1.0, 0.0).astype(BF16)
    keep2 = jnp.where(is_map1, 0.0, 1.0).astype(BF16)
    q_st = []
    for s in range(N_STREAMS):
        qt = q_ref[0, s]
        q_st.append(jnp.concatenate([qt * keep1, qt * keep2], axis=1))

    def score_matmul(s, j, own):
        half = TK // 2

        def keys(lo, n):
            rows = pl.ds(pl.multiple_of(j * TK + lo, half), n)
            return k_ref[0, rows, s * LANES:(s + 1) * LANES]

        if not own:
            full = jnp.dot(keys(0, TK), q_st[s], preferred_element_type=F32)
            return lambda g, cols: full[g:g + BF16_ROWS, cols]
        top = jnp.dot(keys(0, half), q_st[s], preferred_element_type=F32)
        late = jnp.concatenate([q_st[s][:, m * TQ + half:(m + 1) * TQ] for m in range(2)], axis=1)
        bottom = jnp.dot(keys(half, half), late, preferred_element_type=F32)

        def read(g, cols):
            if g < half:
                return top[g:g + BF16_ROWS, cols]
            m, q0 = divmod(cols.start, TQ)
            c0 = m * half + q0 - half
            return bottom[g - half:g - half + BF16_ROWS, c0:c0 + cols.stop - cols.start]

        return read

    def value_matmul(s, j):
        return jnp.dot(vt_ref[0, s, j], p_ref[s], preferred_element_type=F32)

    def accumulate(s, pv):
        acc_ref[s] = alpha_ref[s] * acc_ref[s] + pv[:V_DIM, :]
        l_ref[s] = alpha_ref[s] * l_ref[s] + pv[V_DIM:V_DIM + 1, :]

    def matmuls(j, own, pending):
        scores, pv = [], {}
        for s in range(N_STREAMS):
            scores.append(score_matmul(s, j, own))
            if pending is not None and s >= 1:
                pv[s - 1] = value_matmul(s - 1, pending)
        if pending is not None:
            pv[N_STREAMS - 1] = value_matmul(N_STREAMS - 1, pending)
        return scores, pv

    upper_half = lax.broadcasted_iota(jnp.int32, (BF16_ROWS, LANES), 1) >= CHUNK

    def key_tile(j, own, pending):
        scores, pv = matmuls(j, own, pending)
        width = LANES if own else 2 * TQ
        lane_blocks = [slice(c, c + width) for c in range(0, 2 * TQ, width)]
        cache = {}

        def block(s, g, cols):
            if (s, g, cols.start) not in cache:
                key_chunk = g // CHUNK
                qry_chunk = (cols.start % TQ) // CHUNK
                if own and qry_chunk + 1 < key_chunk:
                    blk = None
                else:
                    blk = scores[s](g, cols)
                    if own and qry_chunk < key_chunk:
                        blk = jnp.where(upper_half, blk, -jnp.inf)
                cache[s, g, cols.start] = blk
            return cache[s, g, cols.start]

        for s in range(N_STREAMS):
            if pending is not None:
                accumulate(s, pv[s])
            mx = {}
            for g in range(0, TK, BF16_ROWS):
                for cols in lane_blocks:
                    blk = block(s, g, cols)
                    if blk is not None:
                        seen = mx.get(cols.start)
                        mx[cols.start] = blk if seen is None else jnp.maximum(seen, blk)
            m_old = m_ref[s]
            col_max = jnp.concatenate(
                [jnp.max(mx[cols.start], axis=0, keepdims=True) for cols in lane_blocks], axis=1)
            m_new = jnp.maximum(m_old, col_max)
            alpha_ref[s] = jnp.exp2(m_old - m_new)
            m_ref[s] = m_new
            for g in range(0, TK, BF16_ROWS):
                for cols in lane_blocks:
                    blk = block(s, g, cols)
                    if blk is None:
                        p = jnp.zeros((BF16_ROWS, width), BF16)
                    else:
                        p = jnp.exp2(blk - m_new[:, cols]).astype(BF16)
                    p_ref[s, g:g + BF16_ROWS, cols] = p

    m_ref[...] = jnp.full(m_ref.shape, -1e30, F32)
    l_ref[...] = jnp.zeros(l_ref.shape, F32)
    acc_ref[...] = jnp.zeros(acc_ref.shape, F32)
    key_tile(qi, True, None)

    def body(j, pending):
        key_tile(j, False, pending)
        return j

    pending = lax.fori_loop(0, qi, body, qi)
    for s in range(N_STREAMS):
        accumulate(s, value_matmul(s, pending))

    lqk = lqk_ref[...]
    lam = (jnp.exp(jnp.sum(lqk[0:1] * lqk[1:2], axis=1, keepdims=True))
           - jnp.exp(jnp.sum(lqk[2:3] * lqk[3:4], axis=1, keepdims=True))
           + lambda_init)
    for s in range(N_STREAMS):
        acc = acc_ref[s]
        inv_l = 1.0 / l_ref[s]
        o = acc[:, :TQ] * inv_l[:, :TQ] - acc[:, TQ:] * (lam * inv_l[:, TQ:])
        ms = jnp.mean(o * o, axis=0, keepdims=True)
        y = (o * (lax.rsqrt(ms + SUBLN_EPS) * (1.0 - lambda_init))) * g_ref[...]
        o_ref[0, :, s * LANES:(s + 1) * LANES] = y.T.astype(BF16)


def _diff_attn(lqk, g_col, qt, k, vt, lambda_init):
    batch, seq, _ = k.shape
    width = N_STREAMS * LANES
    kernel = functools.partial(_attn_kernel, lambda_init=lambda_init)
    return pl.pallas_call(
        kernel,
        grid=(batch, ATTN_HEADS // N_STREAMS, seq // TQ),
        in_specs=[
            pl.BlockSpec((4, HEAD_DIM), lambda b, h, i: (0, 0)),
            pl.BlockSpec((V_DIM, 1), lambda b, h, i: (0, 0)),
            pl.BlockSpec((1, N_STREAMS, LANES, TQ), lambda b, h, i: (b, h, 0, i)),
            pl.BlockSpec((1, seq, width), lambda b, h, i: (b, 0, h)),
            pl.BlockSpec((1, N_STREAMS, seq // TK, VT_ROWS, TK), lambda b, h, i: (b, h, 0, 0, 0)),
        ],
        out_specs=pl.BlockSpec((1, TQ, width), lambda b, h, i: (b, i, h)),
        out_shape=jax.ShapeDtypeStruct((batch, seq, ATTN_HEADS * V_DIM), BF16),
        scratch_shapes=[
            pltpu.VMEM((N_STREAMS, 1, 2 * TQ), F32),
            pltpu.VMEM((N_STREAMS, 1, 2 * TQ), F32),
            pltpu.VMEM((N_STREAMS, V_DIM, 2 * TQ), F32),
            pltpu.VMEM((N_STREAMS, 1, 2 * TQ), F32),
            pltpu.VMEM((N_STREAMS, TK, 2 * TQ), BF16),
        ],
        compiler_params=pltpu.CompilerParams(
            dimension_semantics=("arbitrary", "arbitrary", "arbitrary"),
            vmem_limit_bytes=VMEM_LIMIT),
        name="diff_attn",
    )(lqk, g_col, qt, k, vt)


def _proj_mlp_kernel(x_ref, a_ref, wo_ref, g_ref, w1_ref, w2_ref, gf_ref, out_ref, *, final_norm):
    x1 = x_ref[...] + jnp.dot(a_ref[...], wo_ref[...], preferred_element_type=F32)
    h = _rms(x1, g_ref[...], NORM_EPS).astype(BF16)
    acc = x1
    for c in range(D_FF // FF_CHUNK):
        sl = slice(c * FF_CHUNK, (c + 1) * FF_CHUNK)
        u = jnp.dot(h, w1_ref[:, sl], preferred_element_type=F32)
        u = jnp.square(jnp.maximum(u, 0.0)).astype(BF16)
        acc = acc + jnp.dot(u, w2_ref[sl, :], preferred_element_type=F32)
    if final_norm:
        acc = _rms(acc, gf_ref[...], NORM_EPS)
    out_ref[...] = acc


def _proj_mlp(x2d, a2d, wo, g, w1, w2, gf, final_norm):
    n_tok = x2d.shape[0]
    const = lambda i: (0, 0)
    resident = pl.Buffered(1)
    kernel = functools.partial(_proj_mlp_kernel, final_norm=final_norm)
    return pl.pallas_call(
        kernel,
        grid=(n_tok // TM_MLP,),
        in_specs=[
            pl.BlockSpec((TM_MLP, D_MODEL), lambda i: (i, 0)),
            pl.BlockSpec((TM_MLP, D_MODEL), lambda i: (i, 0)),
            pl.BlockSpec((D_MODEL, D_MODEL), const, pipeline_mode=resident),
            pl.BlockSpec((1, D_MODEL), const),
            pl.BlockSpec((D_MODEL, D_FF), const, pipeline_mode=resident),
            pl.BlockSpec((D_FF, D_MODEL), const, pipeline_mode=resident),
            pl.BlockSpec((1, D_MODEL), const),
        ],
        out_specs=pl.BlockSpec((TM_MLP, D_MODEL), lambda i: (i, 0)),
        out_shape=jax.ShapeDtypeStruct((n_tok, D_MODEL), F32),
        compiler_params=pltpu.CompilerParams(
            dimension_semantics=("arbitrary",), vmem_limit_bytes=VMEM_LIMIT),
        name="proj_mlp_final" if final_norm else "proj_mlp",
    )(x2d, a2d, wo, g, w1, w2, gf)


def _rglru_kernel(x_ref, g_ref, perm_ref, unperm_ref, wx_ref, wy_ref, cw_ref, cb_ref, wa_ref, ba_ref,
                  wi_ref, bi_ref, lam_ref, out_ref, tail_ref, h_ref):
    t = pl.program_id(1)
    n_tail = (CONV_WIDTH - 1) * SUBLANES

    @pl.when(t == 0)
    def _():
        tail_ref[...] = jnp.zeros(tail_ref.shape, F32)
        h_ref[...] = jnp.zeros(h_ref.shape, F32)

    group = lambda v, r: v[r * SUBLANES:(r + 1) * SUBLANES, :]
    sub = lax.broadcasted_iota(jnp.int32, (SUBLANES, D_MODEL), 0)

    def project(s):
        h = _rms(x_ref[s], g_ref[...], NORM_EPS).astype(BF16)
        h = jnp.dot(perm_ref[...], h, preferred_element_type=F32).astype(BF16)
        gate = _gelu_tanh(jnp.dot(h, wy_ref[...], preferred_element_type=F32))
        xp = jnp.dot(h, wx_ref[...], preferred_element_type=F32)
        return gate, xp

    def conv_and_gates(s, xp):
        lead = []
        for k in range(CONV_WIDTH - 1):
            cur = group(xp, REC_STEPS - (CONV_WIDTH - 1) + k)
            prev_last = tail_ref[s, (k + 1) * SUBLANES - 1:(k + 1) * SUBLANES, :]
            lead.append(jnp.where(sub == 0, prev_last, pltpu.roll(cur, 1, 0)))
        tail_ref[s] = xp[TM_REC - n_tail:, :]
        ext = jnp.concatenate(lead + [xp], axis=0)
        cw = cw_ref[...]
        xb = cb_ref[...]
        for j in range(CONV_WIDTH):
            xb = xb + ext[j * SUBLANES:j * SUBLANES + TM_REC, :] * cw[j:j + 1, :]
        xb16 = xb.astype(BF16)

        def block_diag(w_ref):
            return jnp.concatenate(
                [jnp.dot(xb16[:, n * RG_BLOCK:(n + 1) * RG_BLOCK], w_ref[n],
                         preferred_element_type=F32) for n in range(RG_HEADS)], axis=1)

        return xb, block_diag(wa_ref), block_diag(wi_ref)

    def recur(s, gate, xb, ra, ri):
        r = jax.nn.sigmoid(ra + ba_ref[...])
        i = jax.nn.sigmoid(ri + bi_ref[...])
        neg_lam = -lam_ref[...]
        softplus = jnp.maximum(neg_lam, 0.0) + jnp.log1p(jnp.exp(-jnp.abs(neg_lam)))
        rate = (-RG_C * math.log2(math.e)) * softplus
        a = jnp.exp2(r * rate)
        y = 1.0 - a * a
        u = (y * lax.rsqrt(jnp.maximum(y, jnp.finfo(F32).tiny))) * (i * xb)
        hz, pz = [group(u, 0)], [group(a, 0)]
        for k in range(1, REC_STEPS):
            a_k = group(a, k)
            hz.append(a_k * hz[-1] + group(u, k))
            pz.append(a_k * pz[-1])
        start = h_ref[s]
        starts = []
        for b in range(SUBLANES):
            starts.append(start)
            start = hz[-1][b:b + 1, :] + pz[-1][b:b + 1, :] * start
        h_ref[s] = start
        starts = jnp.concatenate(starts, axis=0)
        y = jnp.concatenate([(hz[k] + pz[k] * starts) * group(gate, k) for k in range(REC_STEPS)],
                            axis=0).astype(BF16)
        out_ref[s] = jnp.dot(unperm_ref[...], y, preferred_element_type=F32).astype(BF16)

    streams = range(REC_STREAMS)
    proj = [project(s) for s in streams]
    mid = [conv_and_gates(s, proj[s][1]) for s in streams]
    for s in streams:
        recur(s, proj[s][0], *mid[s])


def _rglru(x, g, wx, wy, cw, cb, wa, ba, wi, bi, lam):
    batch, seq, _ = x.shape
    src = (jnp.arange(TM_REC) % SUBLANES) * REC_STEPS + jnp.arange(TM_REC) // SUBLANES
    perm = (src[:, None] == jnp.arange(TM_REC)[None, :]).astype(BF16)
    c2 = lambda b, t: (0, 0)
    c3 = lambda b, t: (0, 0, 0)
    return pl.pallas_call(
        _rglru_kernel,
        grid=(batch // REC_STREAMS, seq // TM_REC),
        in_specs=[
            pl.BlockSpec((REC_STREAMS, TM_REC, D_MODEL), lambda b, t: (b, t, 0)),
            pl.BlockSpec((1, D_MODEL), c2),
            pl.BlockSpec((TM_REC, TM_REC), c2),
            pl.BlockSpec((TM_REC, TM_REC), c2),
            pl.BlockSpec((D_MODEL, D_MODEL), c2),
            pl.BlockSpec((D_MODEL, D_MODEL), c2),
            pl.BlockSpec((CONV_WIDTH, D_MODEL), c2),
            pl.BlockSpec((1, D_MODEL), c2),
            pl.BlockSpec((RG_HEADS, RG_BLOCK, RG_BLOCK), c3),
            pl.BlockSpec((1, D_MODEL), c2),
            pl.BlockSpec((RG_HEADS, RG_BLOCK, RG_BLOCK), c3),
            pl.BlockSpec((1, D_MODEL), c2),
            pl.BlockSpec((1, D_MODEL), c2),
        ],
        out_specs=pl.BlockSpec((REC_STREAMS, TM_REC, D_MODEL), lambda b, t: (b, t, 0)),
        out_shape=jax.ShapeDtypeStruct((batch, seq, D_MODEL), BF16),
        scratch_shapes=[
            pltpu.VMEM((REC_STREAMS, (CONV_WIDTH - 1) * SUBLANES, D_MODEL), F32),
            pltpu.VMEM((REC_STREAMS, 1, D_MODEL), F32),
        ],
        compiler_params=pltpu.CompilerParams(
            dimension_semantics=("arbitrary", "arbitrary"), vmem_limit_bytes=VMEM_LIMIT),
        name="rglru",
    )(x, g, perm, perm.T, wx, wy, cw, cb, wa, ba, wi, bi, lam)


def _rope_tables(seq):
    half = HEAD_DIM // 2
    inv_freq = 1.0 / (ROPE_THETA ** (jnp.arange(0, HEAD_DIM, 2, dtype=F32) / HEAD_DIM))
    ang = jnp.arange(seq, dtype=jnp.int32).astype(F32)[:, None] * inv_freq[None, :]
    cos = jnp.tile(jnp.cos(ang), (1, LANES // half))
    sin = jnp.tile(jnp.sin(ang), (1, LANES // half))
    sign = jnp.where(jnp.arange(LANES) < LANES // 2, -1.0, 1.0).astype(F32)
    return cos, sin * sign[None, :]


def _permute_heads(w):
    d_in = w.shape[0]
    half = HEAD_DIM // 2
    return w.reshape(d_in, ATTN_HEADS, 2, 2, half).transpose(0, 1, 3, 2, 4).reshape(d_in, -1)


def kernel(x, mix_norm_g, mlp_norm_g, attn_w_qkv, attn_w_o, attn_lq1, attn_lk1, attn_lq2, attn_lk2,
           attn_subln_g, rec_w_x, rec_w_y, rec_conv_w, rec_conv_b, rec_w_a, rec_b_a, rec_w_i, rec_b_i,
           rec_lambda, rec_w_o, mlp_w1, mlp_w2, final_norm_g):
    batch, seq, d = x.shape
    x2d = x.reshape(batch * seq, d)
    row = lambda v: v.reshape(1, -1)

    lambda_init = 0.8 - 0.6 * math.exp(-0.3 * 0)
    w_qkv = attn_w_qkv[0]
    wqt = _permute_heads(w_qkv[:, :d]).T.astype(BF16)
    wk = _permute_heads(w_qkv[:, d:2 * d]).astype(BF16)
    wvt = w_qkv[:, 2 * d:].T.astype(BF16)
    cos, sin = _rope_tables(seq)
    qt, k, vt = _qkv_rope(x2d, row(mix_norm_g[0]), wqt, wk, wvt, cos, sin, batch, seq)
    lqk = jnp.stack([attn_lq1[0], attn_lk1[0], attn_lq2[0], attn_lk2[0]])
    o = _diff_attn(lqk, attn_subln_g[0].reshape(V_DIM, 1),
                   qt, k.reshape(batch, seq, d), vt, lambda_init)
    x2d = _proj_mlp(x2d, o.reshape(batch * seq, d), attn_w_o[0].astype(BF16), row(mlp_norm_g[0]),
                    mlp_w1[0].astype(BF16), mlp_w2[0].astype(BF16), row(final_norm_g), False)

    gated = _rglru(x2d.reshape(batch, seq, d), row(mix_norm_g[1]),
                   rec_w_x[0].astype(BF16), rec_w_y[0].astype(BF16), rec_conv_w[0], row(rec_conv_b[0]),
                   rec_w_a[0].astype(BF16), row(rec_b_a[0]), rec_w_i[0].astype(BF16), row(rec_b_i[0]),
                   row(rec_lambda[0]))
    out = _proj_mlp(x2d, gated.reshape(batch * seq, d), rec_w_o[0].astype(BF16), row(mlp_norm_g[1]),
                    mlp_w1[1].astype(BF16), mlp_w2[1].astype(BF16), row(final_norm_g), True)
    return out.reshape(batch, seq, d)
```

```python
import functools
import math

import jax
import jax.numpy as jnp
from jax import lax
from jax.experimental import pallas as pl
from jax.experimental.pallas import tpu as pltpu

F32 = jnp.float32
BF16 = jnp.bfloat16

D_MODEL = 1024
CHUNK = 64
ATTN_HEADS = 8
HEAD_DIM = 64
V_DIM = 128
ROPE_THETA = 10000.0
RG_BLOCK = 256
RG_HEADS = 4
CONV_WIDTH = 4
RG_C = 8.0
D_FF = 4 * D_MODEL
NORM_EPS = 1e-6
SUBLN_EPS = 1e-5

LANES = 128
SUBLANES = 8
VMEM_LIMIT = 56 * 1024 * 1024

TM_QKV = 1024
TQ = 512
TK = TQ
N_STREAMS = 4
BF16_ROWS = 2 * SUBLANES
VT_ROWS = V_DIM + BF16_ROWS
assert LANES == 2 * CHUNK and TK == TQ
TM_MLP = 1024
FF_CHUNK = 1024
TM_REC = 256
REC_STEPS = TM_REC // SUBLANES
REC_STREAMS = 4

NT = (((1,), (1,)), ((), ()))
Q_SCALE = (HEAD_DIM ** -0.5) * math.log2(math.e)


def _rms(x, g, eps):
    ms = jnp.mean(x * x, axis=-1, keepdims=True)
    return (x * lax.rsqrt(ms + eps)) * g


def _gelu_tanh(x):
    c0 = math.sqrt(2.0 / math.pi)
    inner = x * (c0 + (c0 * 0.044715) * (x * x))
    return (0.5 * x) * (1.0 + jnp.tanh(inner))


def _qkv_kernel(x_ref, g_ref, wqt_ref, wk_ref, wvt_ref, cos_ref, sin_ref, cos_t_ref, sin_t_ref,
                qt_ref, k_ref, vt_ref):
    y = _rms(x_ref[...], g_ref[...], NORM_EPS).astype(BF16)
    half = LANES // 2
    qt = lax.dot_general(wqt_ref[...], y, NT, preferred_element_type=F32)
    cos_t = cos_t_ref[...]
    sin_t = sin_t_ref[...]
    for h in range(ATTN_HEADS):
        t = qt[h * LANES:(h + 1) * LANES, :]
        partner = jnp.concatenate([t[half:, :], t[:half, :]], axis=0)
        qt_ref[0, h] = ((t * cos_t + partner * sin_t) * Q_SCALE).astype(BF16)
    cos = cos_ref[...]
    sin = sin_ref[...]
    k = jnp.dot(y, wk_ref[...], preferred_element_type=F32)
    for h in range(ATTN_HEADS):
        sl = slice(h * LANES, (h + 1) * LANES)
        t = k[:, sl]
        k_ref[:, sl] = (t * cos + pltpu.roll(t, half, 1) * sin).astype(BF16)
    vt = lax.dot_general(wvt_ref[...], y, NT, preferred_element_type=F32)
    for h in range(ATTN_HEADS):
        for t in range(TM_QKV // TK):
            vt_ref[0, h, t, :V_DIM, :] = vt[h * V_DIM:(h + 1) * V_DIM, t * TK:(t + 1) * TK].astype(BF16)
            vt_ref[0, h, t, V_DIM:, :] = jnp.ones((VT_ROWS - V_DIM, TK), BF16)


def _qkv_rope(x2d, g, wqt, wk, wvt, cos, sin, batch, seq):
    n_tok = x2d.shape[0]
    tiles_per_seq = seq // TM_QKV
    const = lambda i: (0, 0)
    return pl.pallas_call(
        _qkv_kernel,
        grid=(n_tok // TM_QKV,),
        in_specs=[
            pl.BlockSpec((TM_QKV, D_MODEL), lambda i: (i, 0)),
            pl.BlockSpec((1, D_MODEL), const),
            pl.BlockSpec((D_MODEL, D_MODEL), const),
            pl.BlockSpec((D_MODEL, D_MODEL), const),
            pl.BlockSpec((D_MODEL, D_MODEL), const),
            pl.BlockSpec((TM_QKV, LANES), lambda i: (i % tiles_per_seq, 0)),
            pl.BlockSpec((TM_QKV, LANES), lambda i: (i % tiles_per_seq, 0)),
            pl.BlockSpec((LANES, TM_QKV), lambda i: (0, i % tiles_per_seq)),
            pl.BlockSpec((LANES, TM_QKV), lambda i: (0, i % tiles_per_seq)),
        ],
        out_specs=[
            pl.BlockSpec((1, ATTN_HEADS, LANES, TM_QKV),
                         lambda i: (i // tiles_per_seq, 0, 0, i % tiles_per_seq)),
            pl.BlockSpec((TM_QKV, D_MODEL), lambda i: (i, 0)),
            pl.BlockSpec((1, ATTN_HEADS, TM_QKV // TK, VT_ROWS, TK),
                         lambda i: (i // tiles_per_seq, 0, i % tiles_per_seq, 0, 0)),
        ],
        out_shape=[
            jax.ShapeDtypeStruct((batch, ATTN_HEADS, LANES, seq), BF16),
            jax.ShapeDtypeStruct((n_tok, D_MODEL), BF16),
            jax.ShapeDtypeStruct((batch, ATTN_HEADS, seq // TK, VT_ROWS, TK), BF16),
        ],
        compiler_params=pltpu.CompilerParams(
            dimension_semantics=("arbitrary",), vmem_limit_bytes=VMEM_LIMIT),
        name="qkv_rope",
    )(x2d, g, wqt, wk, wvt, cos, sin, cos.T, sin.T)


def _attn_kernel(lqk_ref, g_ref, q_ref, k_ref, vt_ref, o_ref, m_ref, l_ref, acc_ref, alpha_ref, p_ref,
                 *, lambda_init):
    qi = pl.program_id(2)

    feature = lax.broadcasted_iota(jnp.int32, (LANES, TQ), 0)
    is_map1 = (feature // (HEAD_DIM // 2)) % 2 == 0
    keep1 = jnp.where(is_map1, 1.0, 0.0).astype(BF16)
    keep2 = jnp.where(is_map1, 0.0, 1.0).astype(BF16)
    q_st = []
    for s in range(N_STREAMS):
        qt = q_ref[0, s]
        q_st.append(jnp.concatenate([qt * keep1, qt * keep2], axis=1))

    def score_matmul(s, j, own):
        half = TK // 2

        def keys(lo, n):
            rows = pl.ds(pl.multiple_of(j * TK + lo, half), n)
            return k_ref[0, rows, s * LANES:(s + 1) * LANES]

        if not own:
            full = jnp.dot(keys(0, TK), q_st[s], preferred_element_type=F32)
            return lambda g, cols: full[g:g + BF16_ROWS, cols]
        top = jnp.dot(keys(0, half), q_st[s], preferred_element_type=F32)
        late = jnp.concatenate([q_st[s][:, m * TQ + half:(m + 1) * TQ] for m in range(2)], axis=1)
        bottom = jnp.dot(keys(half, half), late, preferred_element_type=F32)

        def read(g, cols):
            if g < half:
                return top[g:g + BF16_ROWS, cols]
            m, q0 = divmod(cols.start, TQ)
            c0 = m * half + q0 - half
            return bottom[g - half:g - half + BF16_ROWS, c0:c0 + cols.stop - cols.start]

        return read

    def value_matmul(s, j):
        return jnp.dot(vt_ref[0, s, j], p_ref[s], preferred_element_type=F32)

    def accumulate(s, pv):
        acc_ref[s] = alpha_ref[s] * acc_ref[s] + pv[:V_DIM, :]
        l_ref[s] = alpha_ref[s] * l_ref[s] + pv[V_DIM:V_DIM + 1, :]

    def matmuls(j, own, pending):
        scores, pv = [], {}
        for s in range(N_STREAMS):
            scores.append(score_matmul(s, j, own))
            if pending is not None and s >= 1:
                pv[s - 1] = value_matmul(s - 1, pending)
        if pending is not None:
            pv[N_STREAMS - 1] = value_matmul(N_STREAMS - 1, pending)
        return scores, pv

    upper_half = lax.broadcasted_iota(jnp.int32, (BF16_ROWS, LANES), 1) >= CHUNK

    def key_tile(j, own, pending):
        scores, pv = matmuls(j, own, pending)
        width = LANES if own else 2 * TQ
        lane_blocks = [slice(c, c + width) for c in range(0, 2 * TQ, width)]
        cache = {}

        def block(s, g, cols):
            if (s, g, cols.start) not in cache:
                key_chunk = g // CHUNK
                qry_chunk = (cols.start % TQ) // CHUNK
                if own and qry_chunk + 1 < key_chunk:
                    blk = None
                else:
                    blk = scores[s](g, cols)
                    if own and qry_chunk < key_chunk:
                        blk = jnp.where(upper_half, blk, -jnp.inf)
                cache[s, g, cols.start] = blk
            return cache[s, g, cols.start]

        for s in range(N_STREAMS):
            if pending is not None:
                accumulate(s, pv[s])
            mx = {}
            for g in range(0, TK, BF16_ROWS):
                for cols in lane_blocks:
                    blk = block(s, g, cols)
                    if blk is not None:
                        seen = mx.get(cols.start)
                        mx[cols.start] = blk if seen is None else jnp.maximum(seen, blk)
            m_old = m_ref[s]
            col_max = jnp.concatenate(
                [jnp.max(mx[cols.start], axis=0, keepdims=True) for cols in lane_blocks], axis=1)
            m_new = jnp.maximum(m_old, col_max)
            alpha_ref[s] = jnp.exp2(m_old - m_new)
            m_ref[s] = m_new
            for g in range(0, TK, BF16_ROWS):
                for cols in lane_blocks:
                    blk = block(s, g, cols)
                    if blk is None:
                        p = jnp.zeros((BF16_ROWS, width), BF16)
                    else:
                        p = jnp.exp2(blk - m_new[:, cols]).astype(BF16)
                    p_ref[s, g:g + BF16_ROWS, cols] = p

    m_ref[...] = jnp.full(m_ref.shape, -1e30, F32)
    l_ref[...] = jnp.zeros(l_ref.shape, F32)
    acc_ref[...] = jnp.zeros(acc_ref.shape, F32)
    key_tile(qi, True, None)

    def body(j, pending):
        key_tile(j, False, pending)
        return j

    pending = lax.fori_loop(0, qi, body, qi)
    for s in range(N_STREAMS):
        accumulate(s, value_matmul(s, pending))

    lqk = lqk_ref[...]
    lam = (jnp.exp(jnp.sum(lqk[0:1] * lqk[1:2], axis=1, keepdims=True))
           - jnp.exp(jnp.sum(lqk[2:3] * lqk[3:4], axis=1, keepdims=True))
           + lambda_init)
    for s in range(N_STREAMS):
        acc = acc_ref[s]
        inv_l = 1.0 / l_ref[s]
        o = acc[:, :TQ] * inv_l[:, :TQ] - acc[:, TQ:] * (lam * inv_l[:, TQ:])
        ms = jnp.mean(o * o, axis=0, keepdims=True)
        y = (o * (lax.rsqrt(ms + SUBLN_EPS) * (1.0 - lambda_init))) * g_ref[...]
        o_ref[0, :, s * LANES:(s + 1) * LANES] = y.T.astype(BF16)


def _diff_attn(lqk, g_col, qt, k, vt, lambda_init):
    batch, seq, _ = k.shape
    width = N_STREAMS * LANES
    kernel = functools.partial(_attn_kernel, lambda_init=lambda_init)
    return pl.pallas_call(
        kernel,
        grid=(batch, ATTN_HEADS // N_STREAMS, seq // TQ),
        in_specs=[
            pl.BlockSpec((4, HEAD_DIM), lambda b, h, i: (0, 0)),
            pl.BlockSpec((V_DIM, 1), lambda b, h, i: (0, 0)),
            pl.BlockSpec((1, N_STREAMS, LANES, TQ), lambda b, h, i: (b, h, 0, i)),
            pl.BlockSpec((1, seq, width), lambda b, h, i: (b, 0, h)),
            pl.BlockSpec((1, N_STREAMS, seq // TK, VT_ROWS, TK), lambda b, h, i: (b, h, 0, 0, 0)),
        ],
        out_specs=pl.BlockSpec((1, TQ, width), lambda b, h, i: (b, i, h)),
        out_shape=jax.ShapeDtypeStruct((batch, seq, ATTN_HEADS * V_DIM), BF16),
        scratch_shapes=[
            pltpu.VMEM((N_STREAMS, 1, 2 * TQ), F32),
            pltpu.VMEM((N_STREAMS, 1, 2 * TQ), F32),
            pltpu.VMEM((N_STREAMS, V_DIM, 2 * TQ), F32),
            pltpu.VMEM((N_STREAMS, 1, 2 * TQ), F32),
            pltpu.VMEM((N_STREAMS, TK, 2 * TQ), BF16),
        ],
        compiler_params=pltpu.CompilerParams(
            dimension_semantics=("arbitrary", "arbitrary", "arbitrary"),
            vmem_limit_bytes=VMEM_LIMIT),
        name="diff_attn",
    )(lqk, g_col, qt, k, vt)


def _proj_mlp_kernel(x_ref, a_ref, wo_ref, g_ref, w1_ref, w2_ref, gf_ref, out_ref, *, final_norm):
    x1 = x_ref[...] + jnp.dot(a_ref[...], wo_ref[...], preferred_element_type=F32)
    h = _rms(x1, g_ref[...], NORM_EPS).astype(BF16)
    acc = x1
    for c in range(D_FF // FF_CHUNK):
        sl = slice(c * FF_CHUNK, (c + 1) * FF_CHUNK)
        u = jnp.dot(h, w1_ref[:, sl], preferred_element_type=F32)
        u = jnp.square(jnp.maximum(u, 0.0)).astype(BF16)
        acc = acc + jnp.dot(u, w2_ref[sl, :], preferred_element_type=F32)
    if final_norm:
        acc = _rms(acc, gf_ref[...], NORM_EPS)
    out_ref[...] = acc


def _proj_mlp(x2d, a2d, wo, g, w1, w2, gf, final_norm):
    n_tok = x2d.shape[0]
    const = lambda i: (0, 0)
    resident = pl.Buffered(1)
    kernel = functools.partial(_proj_mlp_kernel, final_norm=final_norm)
    return pl.pallas_call(
        kernel,
        grid=(n_tok // TM_MLP,),
        in_specs=[
            pl.BlockSpec((TM_MLP, D_MODEL), lambda i: (i, 0)),
            pl.BlockSpec((TM_MLP, D_MODEL), lambda i: (i, 0)),
            pl.BlockSpec((D_MODEL, D_MODEL), const, pipeline_mode=resident),
            pl.BlockSpec((1, D_MODEL), const),
            pl.BlockSpec((D_MODEL, D_FF), const, pipeline_mode=resident),
            pl.BlockSpec((D_FF, D_MODEL), const, pipeline_mode=resident),
            pl.BlockSpec((1, D_MODEL), const),
        ],
        out_specs=pl.BlockSpec((TM_MLP, D_MODEL), lambda i: (i, 0)),
        out_shape=jax.ShapeDtypeStruct((n_tok, D_MODEL), F32),
        compiler_params=pltpu.CompilerParams(
            dimension_semantics=("arbitrary",), vmem_limit_bytes=VMEM_LIMIT),
        name="proj_mlp_final" if final_norm else "proj_mlp",
    )(x2d, a2d, wo, g, w1, w2, gf)


def _rglru_kernel(x_ref, g_ref, perm_ref, unperm_ref, wx_ref, wy_ref, cw_ref, cb_ref, wa_ref, ba_ref,
                  wi_ref, bi_ref, lam_ref, out_ref, tail_ref, h_ref):
    t = pl.program_id(1)
    n_tail = (CONV_WIDTH - 1) * SUBLANES

    @pl.when(t == 0)
    def _():
        tail_ref[...] = jnp.zeros(tail_ref.shape, F32)
        h_ref[...] = jnp.zeros(h_ref.shape, F32)

    group = lambda v, r: v[r * SUBLANES:(r + 1) * SUBLANES, :]
    sub = lax.broadcasted_iota(jnp.int32, (SUBLANES, D_MODEL), 0)

    def project(s):
        h = _rms(x_ref[s], g_ref[...], NORM_EPS).astype(BF16)
        h = jnp.dot(perm_ref[...], h, preferred_element_type=F32).astype(BF16)
        gate = _gelu_tanh(jnp.dot(h, wy_ref[...], preferred_element_type=F32))
        xp = jnp.dot(h, wx_ref[...], preferred_element_type=F32)
        return gate, xp

    def conv_and_gates(s, xp):
        lead = []
        for k in range(CONV_WIDTH - 1):
            cur = group(xp, REC_STEPS - (CONV_WIDTH - 1) + k)
            prev_last = tail_ref[s, (k + 1) * SUBLANES - 1:(k + 1) * SUBLANES, :]
            lead.append(jnp.where(sub == 0, prev_last, pltpu.roll(cur, 1, 0)))
        tail_ref[s] = xp[TM_REC - n_tail:, :]
        ext = jnp.concatenate(lead + [xp], axis=0)
        cw = cw_ref[...]
        xb = cb_ref[...]
        for j in range(CONV_WIDTH):
            xb = xb + ext[j * SUBLANES:j * SUBLANES + TM_REC, :] * cw[j:j + 1, :]
        xb16 = xb.astype(BF16)

        def block_diag(w_ref):
            return jnp.concatenate(
                [jnp.dot(xb16[:, n * RG_BLOCK:(n + 1) * RG_BLOCK], w_ref[n],
                         preferred_element_type=F32) for n in range(RG_HEADS)], axis=1)

        return xb, block_diag(wa_ref), block_diag(wi_ref)

    def recur(s, gate, xb, ra, ri):
        r = jax.nn.sigmoid(ra + ba_ref[...])
        i = jax.nn.sigmoid(ri + bi_ref[...])
        neg_lam = -lam_ref[...]
        softplus = jnp.maximum(neg_lam, 0.0) + jnp.log1p(jnp.exp(-jnp.abs(neg_lam)))
        rate = (-RG_C * math.log2(math.e)) * softplus
        a = jnp.exp2(r * rate)
        y = 1.0 - a * a
        u = (y * lax.rsqrt(jnp.maximum(y, jnp.finfo(F32).tiny))) * (i * xb)
        hz, pz = [group(u, 0)], [group(a, 0)]
        for k in range(1, REC_STEPS):
            a_k = group(a, k)
            hz.append(a_k * hz[-1] + group(u, k))
            pz.append(a_k * pz[-1])
        start = h_ref[s]
        starts = []
        for b in range(SUBLANES):
            starts.append(start)
            start = hz[-1][b:b + 1, :] + pz[-1][b:b + 1, :] * start
        h_ref[s] = start
        starts = jnp.concatenate(starts, axis=0)
        y = jnp.concatenate([(hz[k] + pz[k] * starts) * group(gate, k) for k in range(REC_STEPS)],
                            axis=0).astype(BF16)
        out_ref[s] = jnp.dot(unperm_ref[...], y, preferred_element_type=F32).astype(BF16)

    streams = range(REC_STREAMS)
    proj = [project(s) for s in streams]
    mid = [conv_and_gates(s, proj[s][1]) for s in streams]
    for s in streams:
        recur(s, proj[s][0], *mid[s])


def _rglru(x, g, wx, wy, cw, cb, wa, ba, wi, bi, lam):
    batch, seq, _ = x.shape
    src = (jnp.arange(TM_REC) % SUBLANES) * REC_STEPS + jnp.arange(TM_REC) // SUBLANES
    perm = (src[:, None] == jnp.arange(TM_REC)[None, :]).astype(BF16)
    c2 = lambda b, t: (0, 0)
    c3 = lambda b, t: (0, 0, 0)
    return pl.pallas_call(
        _rglru_kernel,
        grid=(batch // REC_STREAMS, seq // TM_REC),
        in_specs=[
            pl.BlockSpec((REC_STREAMS, TM_REC, D_MODEL), lambda b, t: (b, t, 0)),
            pl.BlockSpec((1, D_MODEL), c2),
            pl.BlockSpec((TM_REC, TM_REC), c2),
            pl.BlockSpec((TM_REC, TM_REC), c2),
            pl.BlockSpec((D_MODEL, D_MODEL), c2),
            pl.BlockSpec((D_MODEL, D_MODEL), c2),
            pl.BlockSpec((CONV_WIDTH, D_MODEL), c2),
            pl.BlockSpec((1, D_MODEL), c2),
            pl.BlockSpec((RG_HEADS, RG_BLOCK, RG_BLOCK), c3),
            pl.BlockSpec((1, D_MODEL), c2),
            pl.BlockSpec((RG_HEADS, RG_BLOCK, RG_BLOCK), c3),
            pl.BlockSpec((1, D_MODEL), c2),
            pl.BlockSpec((1, D_MODEL), c2),
        ],
        out_specs=pl.BlockSpec((REC_STREAMS, TM_REC, D_MODEL), lambda b, t: (b, t, 0)),
        out_shape=jax.ShapeDtypeStruct((batch, seq, D_MODEL), BF16),
        scratch_shapes=[
            pltpu.VMEM((REC_STREAMS, (CONV_WIDTH - 1) * SUBLANES, D_MODEL), F32),
            pltpu.VMEM((REC_STREAMS, 1, D_MODEL), F32),
        ],
        compiler_params=pltpu.CompilerParams(
            dimension_semantics=("arbitrary", "arbitrary"), vmem_limit_bytes=VMEM_LIMIT),
        name="rglru",
    )(x, g, perm, perm.T, wx, wy, cw, cb, wa, ba, wi, bi, lam)


def _rope_tables(seq):
    half = HEAD_DIM // 2
    inv_freq = 1.0 / (ROPE_THETA ** (jnp.arange(0, HEAD_DIM, 2, dtype=F32) / HEAD_DIM))
    ang = jnp.arange(seq, dtype=jnp.int32).astype(F32)[:, None] * inv_freq[None, :]
    cos = jnp.tile(jnp.cos(ang), (1, LANES // half))
    sin = jnp.tile(jnp.sin(ang), (1, LANES // half))
    sign = jnp.where(jnp.arange(LANES) < LANES // 2, -1.0, 1.0).astype(F32)
    return cos, sin * sign[None, :]


def _permute_heads(w):
    d_in = w.shape[0]
    half = HEAD_DIM // 2
    return w.reshape(d_in, ATTN_HEADS, 2, 2, half).transpose(0, 1, 3, 2, 4).reshape(d_in, -1)


def kernel(x, mix_norm_g, mlp_norm_g, attn_w_qkv, attn_w_o, attn_lq1, attn_lk1, attn_lq2, attn_lk2,
           attn_subln_g, rec_w_x, rec_w_y, rec_conv_w, rec_conv_b, rec_w_a, rec_b_a, rec_w_i, rec_b_i,
           rec_lambda, rec_w_o, mlp_w1, mlp_w2, final_norm_g):
    batch, seq, d = x.shape
    x2d = x.reshape(batch * seq, d)
    row = lambda v: v.reshape(1, -1)

    lambda_init = 0.8 - 0.6 * math.exp(-0.3 * 0)
    w_qkv = attn_w_qkv[0]
    wqt = _permute_heads(w_qkv[:, :d]).T.astype(BF16)
    wk = _permute_heads(w_qkv[:, d:2 * d]).astype(BF16)
    wvt = w_qkv[:, 2 * d:].T.astype(BF16)
    cos, sin = _rope_tables(seq)
    qt, k, vt = _qkv_rope(x2d, row(mix_norm_g[0]), wqt, wk, wvt, cos, sin, batch, seq)
    lqk = jnp.stack([attn_lq1[0], attn_lk1[0], attn_lq2[0], attn_lk2[0]])
    o = _diff_attn(lqk, attn_subln_g[0].reshape(V_DIM, 1),
                   qt, k.reshape(batch, seq, d), vt, lambda_init)
    x2d = _proj_mlp(x2d, o.reshape(batch * seq, d), attn_w_o[0].astype(BF16), row(mlp_norm_g[0]),
                    mlp_w1[0].astype(BF16), mlp_w2[0].astype(BF16), row(final_norm_g), False)

    gated = _rglru(x2d.reshape(batch, seq, d), row(mix_norm_g[1]),
                   rec_w_x[0].astype(BF16), rec_w_y[0].astype(BF16), rec_conv_w[0], row(rec_conv_b[0]),
                   rec_w_a[0].astype(BF16), row(rec_b_a[0]), rec_w_i[0].astype(BF16), row(rec_b_i[0]),
                   row(rec_lambda[0]))
    out = _proj_mlp(x2d, gated.reshape(batch * seq, d), rec_w_o[0].astype(BF16), row(mlp_norm_g[1]),
                    mlp_w1[1].astype(BF16), mlp_w2[1].astype(BF16), row(final_norm_g), True)
    return out.reshape(batch, seq, d)
```

```python
import functools
import math

import jax
import jax.numpy as jnp
from jax import lax
from jax.experimental import pallas as pl
from jax.experimental.pallas import tpu as pltpu

F32 = jnp.float32
BF16 = jnp.bfloat16

D_MODEL = 1024
CHUNK = 64
ATTN_HEADS = 8
HEAD_DIM = 64
V_DIM = 128
ROPE_THETA = 10000.0
RG_BLOCK = 256
RG_HEADS = 4
CONV_WIDTH = 4
RG_C = 8.0
D_FF = 4 * D_MODEL
NORM_EPS = 1e-6
SUBLN_EPS = 1e-5

LANES = 128
SUBLANES = 8
VMEM_LIMIT = 56 * 1024 * 1024

TM_QKV = 1024
TQ = 512
TK = TQ
N_STREAMS = 4
BF16_ROWS = 2 * SUBLANES
VT_ROWS = V_DIM + BF16_ROWS
assert LANES == 2 * CHUNK and TK == TQ
TM_MLP = 1024
FF_CHUNK = 1024
TM_REC = 256
REC_STEPS = TM_REC // SUBLANES
REC_STREAMS = 4
REC_COLS = 512

NT = (((1,), (1,)), ((), ()))
Q_SCALE = (HEAD_DIM ** -0.5) * math.log2(math.e)


def _rms(x, g, eps):
    ms = jnp.mean(x * x, axis=-1, keepdims=True)
    return (x * lax.rsqrt(ms + eps)) * g


def _gelu_tanh(x):
    c0 = math.sqrt(2.0 / math.pi)
    inner = x * (c0 + (c0 * 0.044715) * (x * x))
    return (0.5 * x) * (1.0 + jnp.tanh(inner))


def _qkv_kernel(x_ref, g_ref, wqt_ref, wk_ref, wvt_ref, cos_ref, sin_ref, cos_t_ref, sin_t_ref,
                qt_ref, k_ref, vt_ref):
    y = _rms(x_ref[...], g_ref[...], NORM_EPS).astype(BF16)
    half = LANES // 2
    qt = lax.dot_general(wqt_ref[...], y, NT, preferred_element_type=F32)
    cos_t = cos_t_ref[...]
    sin_t = sin_t_ref[...]
    for h in range(ATTN_HEADS):
        t = qt[h * LANES:(h + 1) * LANES, :]
        partner = jnp.concatenate([t[half:, :], t[:half, :]], axis=0)
        qt_ref[0, h] = ((t * cos_t + partner * sin_t) * Q_SCALE).astype(BF16)
    cos = cos_ref[...]
    sin = sin_ref[...]
    k = jnp.dot(y, wk_ref[...], preferred_element_type=F32)
    for h in range(ATTN_HEADS):
        sl = slice(h * LANES, (h + 1) * LANES)
        t = k[:, sl]
        k_ref[:, sl] = (t * cos + pltpu.roll(t, half, 1) * sin).astype(BF16)
    vt = lax.dot_general(wvt_ref[...], y, NT, preferred_element_type=F32)
    for h in range(ATTN_HEADS):
        for t in range(TM_QKV // TK):
            vt_ref[0, h, t, :V_DIM, :] = vt[h * V_DIM:(h + 1) * V_DIM, t * TK:(t + 1) * TK].astype(BF16)
            vt_ref[0, h, t, V_DIM:, :] = jnp.ones((VT_ROWS - V_DIM, TK), BF16)


def _qkv_rope(x2d, g, wqt, wk, wvt, cos, sin, batch, seq):
    n_tok = x2d.shape[0]
    tiles_per_seq = seq // TM_QKV
    const = lambda i: (0, 0)
    return pl.pallas_call(
        _qkv_kernel,
        grid=(n_tok // TM_QKV,),
        in_specs=[
            pl.BlockSpec((TM_QKV, D_MODEL), lambda i: (i, 0)),
            pl.BlockSpec((1, D_MODEL), const),
            pl.BlockSpec((D_MODEL, D_MODEL), const),
            pl.BlockSpec((D_MODEL, D_MODEL), const),
            pl.BlockSpec((D_MODEL, D_MODEL), const),
            pl.BlockSpec((TM_QKV, LANES), lambda i: (i % tiles_per_seq, 0)),
            pl.BlockSpec((TM_QKV, LANES), lambda i: (i % tiles_per_seq, 0)),
            pl.BlockSpec((LANES, TM_QKV), lambda i: (0, i % tiles_per_seq)),
            pl.BlockSpec((LANES, TM_QKV), lambda i: (0, i % tiles_per_seq)),
        ],
        out_specs=[
            pl.BlockSpec((1, ATTN_HEADS, LANES, TM_QKV),
                         lambda i: (i // tiles_per_seq, 0, 0, i % tiles_per_seq)),
            pl.BlockSpec((TM_QKV, D_MODEL), lambda i: (i, 0)),
            pl.BlockSpec((1, ATTN_HEADS, TM_QKV // TK, VT_ROWS, TK),
                         lambda i: (i // tiles_per_seq, 0, i % tiles_per_seq, 0, 0)),
        ],
        out_shape=[
            jax.ShapeDtypeStruct((batch, ATTN_HEADS, LANES, seq), BF16),
            jax.ShapeDtypeStruct((n_tok, D_MODEL), BF16),
            jax.ShapeDtypeStruct((batch, ATTN_HEADS, seq // TK, VT_ROWS, TK), BF16),
        ],
        compiler_params=pltpu.CompilerParams(
            dimension_semantics=("arbitrary",), vmem_limit_bytes=VMEM_LIMIT),
        name="qkv_rope",
    )(x2d, g, wqt, wk, wvt, cos, sin, cos.T, sin.T)


def _attn_kernel(lqk_ref, g_ref, q_ref, k_ref, vt_ref, o_ref, m_ref, l_ref, acc_ref, alpha_ref, p_ref,
                 *, lambda_init):
    qi = pl.program_id(2)

    feature = lax.broadcasted_iota(jnp.int32, (LANES, TQ), 0)
    is_map1 = (feature // (HEAD_DIM // 2)) % 2 == 0
    keep1 = jnp.where(is_map1, 1.0, 0.0).astype(BF16)
    keep2 = jnp.where(is_map1, 0.0, 1.0).astype(BF16)
    q_st = []
    for s in range(N_STREAMS):
        qt = q_ref[0, s]
        q_st.append(jnp.concatenate([qt * keep1, qt * keep2], axis=1))

    def score_matmul(s, j, own):
        half = TK // 2

        def keys(lo, n):
            rows = pl.ds(pl.multiple_of(j * TK + lo, half), n)
            return k_ref[0, rows, s * LANES:(s + 1) * LANES]

        if not own:
            full = jnp.dot(keys(0, TK), q_st[s], preferred_element_type=F32)
            return lambda g, cols: full[g:g + BF16_ROWS, cols]
        top = jnp.dot(keys(0, half), q_st[s], preferred_element_type=F32)
        late = jnp.concatenate([q_st[s][:, m * TQ + half:(m + 1) * TQ] for m in range(2)], axis=1)
        bottom = jnp.dot(keys(half, half), late, preferred_element_type=F32)

        def read(g, cols):
            if g < half:
                return top[g:g + BF16_ROWS, cols]
            m, q0 = divmod(cols.start, TQ)
            c0 = m * half + q0 - half
            return bottom[g - half:g - half + BF16_ROWS, c0:c0 + cols.stop - cols.start]

        return read

    def value_matmul(s, j):
        return jnp.dot(vt_ref[0, s, j], p_ref[s], preferred_element_type=F32)

    def accumulate(s, pv):
        acc_ref[s] = alpha_ref[s] * acc_ref[s] + pv[:V_DIM, :]
        l_ref[s] = alpha_ref[s] * l_ref[s] + pv[V_DIM:V_DIM + 1, :]

    def matmuls(j, own, pending):
        scores, pv = [], {}
        for s in range(N_STREAMS):
            scores.append(score_matmul(s, j, own))
            if pending is not None and s >= 1:
                pv[s - 1] = value_matmul(s - 1, pending)
        if pending is not None:
            pv[N_STREAMS - 1] = value_matmul(N_STREAMS - 1, pending)
        return scores, pv

    upper_half = lax.broadcasted_iota(jnp.int32, (BF16_ROWS, LANES), 1) >= CHUNK

    def key_tile(j, own, pending):
        scores, pv = matmuls(j, own, pending)
        width = LANES if own else 2 * TQ
        lane_blocks = [slice(c, c + width) for c in range(0, 2 * TQ, width)]
        cache = {}

        def block(s, g, cols):
            if (s, g, cols.start) not in cache:
                key_chunk = g // CHUNK
                qry_chunk = (cols.start % TQ) // CHUNK
                if own and qry_chunk + 1 < key_chunk:
                    blk = None
                else:
                    blk = scores[s](g, cols)
                    if own and qry_chunk < key_chunk:
                        blk = jnp.where(upper_half, blk, -jnp.inf)
                cache[s, g, cols.start] = blk
            return cache[s, g, cols.start]

        for s in range(N_STREAMS):
            if pending is not None:
                accumulate(s, pv[s])
            mx = {}
            for g in range(0, TK, BF16_ROWS):
                for cols in lane_blocks:
                    blk = block(s, g, cols)
                    if blk is not None:
                        seen = mx.get(cols.start)
                        mx[cols.start] = blk if seen is None else jnp.maximum(seen, blk)
            m_old = m_ref[s]
            col_max = jnp.concatenate(
                [jnp.max(mx[cols.start], axis=0, keepdims=True) for cols in lane_blocks], axis=1)
            m_new = jnp.maximum(m_old, col_max)
            alpha_ref[s] = jnp.exp2(m_old - m_new)
            m_ref[s] = m_new
            for g in range(0, TK, BF16_ROWS):
                for cols in lane_blocks:
                    blk = block(s, g, cols)
                    if blk is None:
                        p = jnp.zeros((BF16_ROWS, width), BF16)
                    else:
                        p = jnp.exp2(blk - m_new[:, cols]).astype(BF16)
                    p_ref[s, g:g + BF16_ROWS, cols] = p

    m_ref[...] = jnp.full(m_ref.shape, -1e30, F32)
    l_ref[...] = jnp.zeros(l_ref.shape, F32)
    acc_ref[...] = jnp.zeros(acc_ref.shape, F32)
    key_tile(qi, True, None)

    def body(j, pending):
        key_tile(j, False, pending)
        return j

    pending = lax.fori_loop(0, qi, body, qi)
    for s in range(N_STREAMS):
        accumulate(s, value_matmul(s, pending))

    lqk = lqk_ref[...]
    lam = (jnp.exp(jnp.sum(lqk[0:1] * lqk[1:2], axis=1, keepdims=True))
           - jnp.exp(jnp.sum(lqk[2:3] * lqk[3:4], axis=1, keepdims=True))
           + lambda_init)
    for s in range(N_STREAMS):
        acc = acc_ref[s]
        inv_l = 1.0 / l_ref[s]
        o = acc[:, :TQ] * inv_l[:, :TQ] - acc[:, TQ:] * (lam * inv_l[:, TQ:])
        ms = jnp.mean(o * o, axis=0, keepdims=True)
        y = (o * (lax.rsqrt(ms + SUBLN_EPS) * (1.0 - lambda_init))) * g_ref[...]
        o_ref[0, :, s * LANES:(s + 1) * LANES] = y.T.astype(BF16)


def _diff_attn(lqk, g_col, qt, k, vt, lambda_init):
    batch, seq, _ = k.shape
    width = N_STREAMS * LANES
    kernel = functools.partial(_attn_kernel, lambda_init=lambda_init)
    return pl.pallas_call(
        kernel,
        grid=(batch, ATTN_HEADS // N_STREAMS, seq // TQ),
        in_specs=[
            pl.BlockSpec((4, HEAD_DIM), lambda b, h, i: (0, 0)),
            pl.BlockSpec((V_DIM, 1), lambda b, h, i: (0, 0)),
            pl.BlockSpec((1, N_STREAMS, LANES, TQ), lambda b, h, i: (b, h, 0, i)),
            pl.BlockSpec((1, seq, width), lambda b, h, i: (b, 0, h)),
            pl.BlockSpec((1, N_STREAMS, seq // TK, VT_ROWS, TK), lambda b, h, i: (b, h, 0, 0, 0)),
        ],
        out_specs=pl.BlockSpec((1, TQ, width), lambda b, h, i: (b, i, h)),
        out_shape=jax.ShapeDtypeStruct((batch, seq, ATTN_HEADS * V_DIM), BF16),
        scratch_shapes=[
            pltpu.VMEM((N_STREAMS, 1, 2 * TQ), F32),
            pltpu.VMEM((N_STREAMS, 1, 2 * TQ), F32),
            pltpu.VMEM((N_STREAMS, V_DIM, 2 * TQ), F32),
            pltpu.VMEM((N_STREAMS, 1, 2 * TQ), F32),
            pltpu.VMEM((N_STREAMS, TK, 2 * TQ), BF16),
        ],
        compiler_params=pltpu.CompilerParams(
            dimension_semantics=("arbitrary", "arbitrary", "arbitrary"),
            vmem_limit_bytes=VMEM_LIMIT),
        name="diff_attn",
    )(lqk, g_col, qt, k, vt)


def _proj_mlp_kernel(x_ref, a_ref, wo_ref, g_ref, w1_ref, w2_ref, gf_ref, out_ref, *, final_norm):
    x1 = x_ref[...] + jnp.dot(a_ref[...], wo_ref[...], preferred_element_type=F32)
    h = _rms(x1, g_ref[...], NORM_EPS).astype(BF16)
    acc = x1
    for c in range(D_FF // FF_CHUNK):
        sl = slice(c * FF_CHUNK, (c + 1) * FF_CHUNK)
        u = jnp.dot(h, w1_ref[:, sl], preferred_element_type=F32)
        u = jnp.square(jnp.maximum(u, 0.0)).astype(BF16)
        acc = acc + jnp.dot(u, w2_ref[sl, :], preferred_element_type=F32)
    if final_norm:
        acc = _rms(acc, gf_ref[...], NORM_EPS)
    out_ref[...] = acc


def _proj_mlp(x2d, a2d, wo, g, w1, w2, gf, final_norm):
    n_tok = x2d.shape[0]
    const = lambda i: (0, 0)
    resident = pl.Buffered(1)
    kernel = functools.partial(_proj_mlp_kernel, final_norm=final_norm)
    return pl.pallas_call(
        kernel,
        grid=(n_tok // TM_MLP,),
        in_specs=[
            pl.BlockSpec((TM_MLP, D_MODEL), lambda i: (i, 0)),
            pl.BlockSpec((TM_MLP, D_MODEL), lambda i: (i, 0)),
            pl.BlockSpec((D_MODEL, D_MODEL), const, pipeline_mode=resident),
            pl.BlockSpec((1, D_MODEL), const),
            pl.BlockSpec((D_MODEL, D_FF), const, pipeline_mode=resident),
            pl.BlockSpec((D_FF, D_MODEL), const, pipeline_mode=resident),
            pl.BlockSpec((1, D_MODEL), const),
        ],
        out_specs=pl.BlockSpec((TM_MLP, D_MODEL), lambda i: (i, 0)),
        out_shape=jax.ShapeDtypeStruct((n_tok, D_MODEL), F32),
        compiler_params=pltpu.CompilerParams(
            dimension_semantics=("arbitrary",), vmem_limit_bytes=VMEM_LIMIT),
        name="proj_mlp_final" if final_norm else "proj_mlp",
    )(x2d, a2d, wo, g, w1, w2, gf)


def _rglru_kernel(x_ref, g_ref, perm_ref, unperm_ref, wx_ref, wy_ref, cw_ref, cb_ref, wa_ref, ba_ref,
                  wi_ref, bi_ref, lam_ref, out_ref, tail_ref, h_ref):
    t = pl.program_id(1)
    n_tail = (CONV_WIDTH - 1) * SUBLANES

    @pl.when(t == 0)
    def _():
        tail_ref[...] = jnp.zeros(tail_ref.shape, F32)
        h_ref[...] = jnp.zeros(h_ref.shape, F32)

    group = lambda v, r: v[r * SUBLANES:(r + 1) * SUBLANES, :]
    sub = lax.broadcasted_iota(jnp.int32, (SUBLANES, D_MODEL), 0)

    def project(s):
        h = _rms(x_ref[s], g_ref[...], NORM_EPS).astype(BF16)
        h = jnp.dot(perm_ref[...], h, preferred_element_type=F32).astype(BF16)
        gate = _gelu_tanh(jnp.dot(h, wy_ref[...], preferred_element_type=F32))
        xp = jnp.dot(h, wx_ref[...], preferred_element_type=F32)
        return gate, xp

    def conv_and_gates(s, xp):
        lead = []
        for k in range(CONV_WIDTH - 1):
            cur = group(xp, REC_STEPS - (CONV_WIDTH - 1) + k)
            prev_last = tail_ref[s, (k + 1) * SUBLANES - 1:(k + 1) * SUBLANES, :]
            lead.append(jnp.where(sub == 0, prev_last, pltpu.roll(cur, 1, 0)))
        tail_ref[s] = xp[TM_REC - n_tail:, :]
        ext = jnp.concatenate(lead + [xp], axis=0)
        cw = cw_ref[...]
        xb = cb_ref[...]
        for j in range(CONV_WIDTH):
            xb = xb + ext[j * SUBLANES:j * SUBLANES + TM_REC, :] * cw[j:j + 1, :]
        xb16 = xb.astype(BF16)

        def block_diag(w_ref):
            return jnp.concatenate(
                [jnp.dot(xb16[:, n * RG_BLOCK:(n + 1) * RG_BLOCK], w_ref[n],
                         preferred_element_type=F32) for n in range(RG_HEADS)], axis=1)

        return xb, block_diag(wa_ref), block_diag(wi_ref)

    def recur(s, gate, xb, ra, ri):
        for c0 in range(0, D_MODEL, REC_COLS):
            cols = slice(c0, c0 + REC_COLS)
            r = jax.nn.sigmoid(ra[:, cols] + ba_ref[:, cols])
            i = jax.nn.sigmoid(ri[:, cols] + bi_ref[:, cols])
            neg_lam = -lam_ref[:, cols]
            softplus = jnp.maximum(neg_lam, 0.0) + jnp.log1p(jnp.exp(-jnp.abs(neg_lam)))
            rate = (-RG_C * math.log2(math.e)) * softplus
            a = jnp.exp2(r * rate)
            y = 1.0 - a * a
            u = (y * lax.rsqrt(jnp.maximum(y, jnp.finfo(F32).tiny))) * (i * xb[:, cols])
            hz, pz = [group(u, 0)], [group(a, 0)]
            for k in range(1, REC_STEPS):
                a_k = group(a, k)
                hz.append(a_k * hz[-1] + group(u, k))
                pz.append(a_k * pz[-1])
            start = h_ref[s, :, cols]
            starts = []
            for b in range(SUBLANES):
                starts.append(start)
                start = hz[-1][b:b + 1, :] + pz[-1][b:b + 1, :] * start
            h_ref[s, :, cols] = start
            starts = jnp.concatenate(starts, axis=0)
            y = jnp.concatenate(
                [(hz[k] + pz[k] * starts) * group(gate[:, cols], k) for k in range(REC_STEPS)],
                axis=0).astype(BF16)
            out_ref[s, :, cols] = jnp.dot(unperm_ref[...], y,
                                          preferred_element_type=F32).astype(BF16)

    streams = range(REC_STREAMS)
    proj = [project(s) for s in streams]
    mid = [conv_and_gates(s, proj[s][1]) for s in streams]
    for s in streams:
        recur(s, proj[s][0], *mid[s])


def _rglru(x, g, wx, wy, cw, cb, wa, ba, wi, bi, lam):
    batch, seq, _ = x.shape
    src = (jnp.arange(TM_REC) % SUBLANES) * REC_STEPS + jnp.arange(TM_REC) // SUBLANES
    perm = (src[:, None] == jnp.arange(TM_REC)[None, :]).astype(BF16)
    c2 = lambda b, t: (0, 0)
    c3 = lambda b, t: (0, 0, 0)
    return pl.pallas_call(
        _rglru_kernel,
        grid=(batch // REC_STREAMS, seq // TM_REC),
        in_specs=[
            pl.BlockSpec((REC_STREAMS, TM_REC, D_MODEL), lambda b, t: (b, t, 0)),
            pl.BlockSpec((1, D_MODEL), c2),
            pl.BlockSpec((TM_REC, TM_REC), c2),
            pl.BlockSpec((TM_REC, TM_REC), c2),
            pl.BlockSpec((D_MODEL, D_MODEL), c2),
            pl.BlockSpec((D_MODEL, D_MODEL), c2),
            pl.BlockSpec((CONV_WIDTH, D_MODEL), c2),
            pl.BlockSpec((1, D_MODEL), c2),
            pl.BlockSpec((RG_HEADS, RG_BLOCK, RG_BLOCK), c3),
            pl.BlockSpec((1, D_MODEL), c2),
            pl.BlockSpec((RG_HEADS, RG_BLOCK, RG_BLOCK), c3),
            pl.BlockSpec((1, D_MODEL), c2),
            pl.BlockSpec((1, D_MODEL), c2),
        ],
        out_specs=pl.BlockSpec((REC_STREAMS, TM_REC, D_MODEL), lambda b, t: (b, t, 0)),
        out_shape=jax.ShapeDtypeStruct((batch, seq, D_MODEL), BF16),
        scratch_shapes=[
            pltpu.VMEM((REC_STREAMS, (CONV_WIDTH - 1) * SUBLANES, D_MODEL), F32),
            pltpu.VMEM((REC_STREAMS, 1, D_MODEL), F32),
        ],
        compiler_params=pltpu.CompilerParams(
            dimension_semantics=("arbitrary", "arbitrary"), vmem_limit_bytes=VMEM_LIMIT),
        name="rglru",
    )(x, g, perm, perm.T, wx, wy, cw, cb, wa, ba, wi, bi, lam)


def _rope_tables(seq):
    half = HEAD_DIM // 2
    inv_freq = 1.0 / (ROPE_THETA ** (jnp.arange(0, HEAD_DIM, 2, dtype=F32) / HEAD_DIM))
    ang = jnp.arange(seq, dtype=jnp.int32).astype(F32)[:, None] * inv_freq[None, :]
    cos = jnp.tile(jnp.cos(ang), (1, LANES // half))
    sin = jnp.tile(jnp.sin(ang), (1, LANES // half))
    sign = jnp.where(jnp.arange(LANES) < LANES // 2, -1.0, 1.0).astype(F32)
    return cos, sin * sign[None, :]


def _permute_heads(w):
    d_in = w.shape[0]
    half = HEAD_DIM // 2
    return w.reshape(d_in, ATTN_HEADS, 2, 2, half).transpose(0, 1, 3, 2, 4).reshape(d_in, -1)


def kernel(x, mix_norm_g, mlp_norm_g, attn_w_qkv, attn_w_o, attn_lq1, attn_lk1, attn_lq2, attn_lk2,
           attn_subln_g, rec_w_x, rec_w_y, rec_conv_w, rec_conv_b, rec_w_a, rec_b_a, rec_w_i, rec_b_i,
           rec_lambda, rec_w_o, mlp_w1, mlp_w2, final_norm_g):
    batch, seq, d = x.shape
    x2d = x.reshape(batch * seq, d)
    row = lambda v: v.reshape(1, -1)

    lambda_init = 0.8 - 0.6 * math.exp(-0.3 * 0)
    w_qkv = attn_w_qkv[0]
    wqt = _permute_heads(w_qkv[:, :d]).T.astype(BF16)
    wk = _permute_heads(w_qkv[:, d:2 * d]).astype(BF16)
    wvt = w_qkv[:, 2 * d:].T.astype(BF16)
    cos, sin = _rope_tables(seq)
    qt, k, vt = _qkv_rope(x2d, row(mix_norm_g[0]), wqt, wk, wvt, cos, sin, batch, seq)
    lqk = jnp.stack([attn_lq1[0], attn_lk1[0], attn_lq2[0], attn_lk2[0]])
    o = _diff_attn(lqk, attn_subln_g[0].reshape(V_DIM, 1),
                   qt, k.reshape(batch, seq, d), vt, lambda_init)
    x2d = _proj_mlp(x2d, o.reshape(batch * seq, d), attn_w_o[0].astype(BF16), row(mlp_norm_g[0]),
                    mlp_w1[0].astype(BF16), mlp_w2[0].astype(BF16), row(final_norm_g), False)

    gated = _rglru(x2d.reshape(batch, seq, d), row(mix_norm_g[1]),
                   rec_w_x[0].astype(BF16), rec_w_y[0].astype(BF16), rec_conv_w[0], row(rec_conv_b[0]),
                   rec_w_a[0].astype(BF16), row(rec_b_a[0]), rec_w_i[0].astype(BF16), row(rec_b_i[0]),
                   row(rec_lambda[0]))
    out = _proj_mlp(x2d, gated.reshape(batch * seq, d), rec_w_o[0].astype(BF16), row(mlp_norm_g[1]),
                    mlp_w1[1].astype(BF16), mlp_w2[1].astype(BF16), row(final_norm_g), True)
    return out.reshape(batch, seq, d)
```
